```python
import numpy as np
import jax, jax.numpy as jnp
from jax import lax

D_MODEL = 1024
BATCH = 8
SEQ = 4096
DEPTH = 4

D_MIX = D_MODEL
N_MIXERS = 4
GROUP_W = D_MIX // N_MIXERS

MLA_HEADS = 4
MLA_V = GROUP_W // MLA_HEADS
MLA_NOPE = MLA_V
MLA_ROPE = MLA_NOPE // 2
MLA_Q_RANK = D_MODEL // 4
MLA_KV_RANK = D_MODEL // 8
ROPE_THETA = 10000.0
MAX_START = 4096

CONV_CH = GROUP_W
CONV_K = 3

GMLP_GROUPS = 4
GMLP_CHUNK = 128

NSA_HEADS = 4
NSA_DH = GROUP_W // NSA_HEADS
CMP_LEN = 32
CMP_STRIDE = 16
CMP_HIDDEN = 4 * NSA_DH
SEL_LEN = 64
N_SELECT = 16
WINDOW = 512
Q_BLOCK = 128
FORCE_SCORE = 1e4

D_FF = 7 * D_MODEL // 2
N_EXPERTS = 8
TOP_K = 2

EPS = 1e-6
NEG = -1e30

IN_SIZES = (MLA_Q_RANK, MLA_KV_RANK, MLA_ROPE,
            CONV_CH, CONV_CH, CONV_CH,
            GROUP_W, GROUP_W,
            NSA_HEADS * NSA_DH,
            NSA_DH, NSA_DH, NSA_DH, NSA_DH, NSA_DH, NSA_DH,
            NSA_HEADS * 3)
D_IN = sum(IN_SIZES)

kernel_name = 'hybrid_mla_conv_gmlp_nsa_moe'


def rms_norm(x, g):
    xf = x.astype(jnp.float32)
    y = xf * lax.rsqrt(jnp.mean(xf * xf, axis=-1, keepdims=True) + EPS)
    return (y * g.astype(jnp.float32)).astype(x.dtype)


def layer_norm(x, g):
    xf = x.astype(jnp.float32)
    mu = jnp.mean(xf, axis=-1, keepdims=True)
    var = jnp.mean(jnp.square(xf - mu), axis=-1, keepdims=True)
    return ((xf - mu) * lax.rsqrt(var + EPS) * g.astype(jnp.float32)).astype(x.dtype)


def masked_softmax(s, mask):
    return jax.nn.softmax(jnp.where(mask, s.astype(jnp.float32), NEG), axis=-1)


def rope(x, positions):
    half = x.shape[-1] // 2
    inv = ROPE_THETA ** (-jnp.arange(half, dtype=jnp.float32) / half)
    ang = positions.astype(jnp.float32)[..., None] * inv
    cos = jnp.cos(ang)[:, :, None, :]
    sin = jnp.sin(ang)[:, :, None, :]
    xf = x.astype(jnp.float32)
    x1, x2 = xf[..., :half], xf[..., half:]
    return jnp.concatenate([x1 * cos - x2 * sin, x1 * sin + x2 * cos], axis=-1).astype(x.dtype)


def causal_attention_blocked(q, k, v, scale):
    B, S, H, dk = q.shape
    nq = S // Q_BLOCK
    qb = q.reshape(B, nq, Q_BLOCK, H, dk).transpose(1, 0, 2, 3, 4)
    k_idx = jnp.arange(S)

    def block(args):
        qi, i = args
        s = jnp.einsum('bqhd,bkhd->bhqk', qi, k).astype(jnp.float32) * scale
        q_idx = i * Q_BLOCK + jnp.arange(Q_BLOCK)
        p = masked_softmax(s, k_idx[None, :] <= q_idx[:, None])
        return jnp.einsum('bhqk,bkhd->bqhd', p.astype(v.dtype), v)

    out = lax.map(block, (qb, jnp.arange(nq)))
    return out.transpose(1, 0, 2, 3, 4).reshape(B, S, H, v.shape[-1])


def mla_mixer(cq, ckv, kr, positions, q_norm, w_uq, kv_norm, w_ukv):
    B, S, _ = cq.shape
    q = (rms_norm(cq, q_norm) @ w_uq).reshape(B, S, MLA_HEADS, MLA_NOPE + MLA_ROPE)
    q = jnp.concatenate([q[..., :MLA_NOPE], rope(q[..., MLA_NOPE:], positions)], axis=-1)
    kv = (rms_norm(ckv, kv_norm) @ w_ukv).reshape(B, S, MLA_HEADS, MLA_NOPE + MLA_V)
    k_rope = jnp.broadcast_to(rope(kr[:, :, None, :], positions), (B, S, MLA_HEADS, MLA_ROPE))
    k = jnp.concatenate([kv[..., :MLA_NOPE], k_rope], axis=-1)
    v = kv[..., MLA_NOPE:]
    o = causal_attention_blocked(q, k, v, (MLA_NOPE + MLA_ROPE) ** -0.5)
    return o.reshape(B, S, MLA_HEADS * MLA_V)


def short_conv_mixer(b_gate, c_gate, h, conv_w):
    u = c_gate * h
    y = lax.conv_general_dilated(u, conv_w[:, None, :].astype(u.dtype), window_strides=(1,),
                                 padding=[(CONV_K - 1, 0)],
                                 dimension_numbers=('NWC', 'WIO', 'NWC'),
                                 feature_group_count=u.shape[-1])
    return b_gate * y


def gmlp_mixer(u, v, ln_g, w_s, b_s):
    B, S, W = u.shape
    u = jax.nn.gelu(u)
    v = layer_norm(jax.nn.gelu(v), ln_g)
    n = S // GMLP_CHUNK
    dg = W // GMLP_GROUPS
    vc = v.reshape(B, n, GMLP_CHUNK, GMLP_GROUPS, dg)
    causal = jnp.tril(jnp.ones((GMLP_CHUNK, GMLP_CHUNK), dtype=bool))
    ws = jnp.where(causal[None], w_s, 0)
    sv = jnp.einsum('gij,bnjgd->bnigd', ws, vc) + b_s.T[None, None, :, :, None]
    return u * sv.reshape(B, S, W)


def compress_blocks(z, pe, w1, w2):
    B, S, d = z.shape
    r = CMP_LEN // CMP_STRIDE
    n = S // CMP_STRIDE
    ch = z.reshape(B, n, CMP_STRIDE, d)
    blocks = jnp.concatenate([ch[:, j:n - r + 1 + j] for j in range(r)], axis=2)
    nc = blocks.shape[1]
    blocks = (blocks + pe).reshape(B, nc, CMP_LEN * d)
    return jax.nn.gelu(blocks @ w1) @ w2


def cmp_to_sel_matrix(S):
    nc = S // CMP_STRIDE - CMP_LEN // CMP_STRIDE + 1
    ns = S // SEL_LEN
    cs = np.arange(nc)[:, None] * CMP_STRIDE
    ss = np.arange(ns)[None, :] * SEL_LEN
    ov = np.clip(np.minimum(cs + CMP_LEN, ss + SEL_LEN) - np.maximum(cs, ss), 0, None)
    return jnp.asarray(ov.astype(np.float32) / np.float32(CMP_LEN))


def nsa_mixer(q, kc_raw, vc_raw, ks, vs, kw, vw, g_logits,
              pe_k, w1_k, w2_k, pe_v, w1_v, w2_v):
    B, S, _ = q.shape
    H, dh = NSA_HEADS, NSA_DH
    scale = dh ** -0.5
    q = q.reshape(B, S, H, dh)
    t_idx = jnp.arange(S)
    nq = S // Q_BLOCK

    kc = compress_blocks(kc_raw, pe_k, w1_k, w2_k)
    vc = compress_blocks(vc_raw, pe_v, w1_v, w2_v)
    nc = kc.shape[1]
    blk_end = jnp.arange(nc) * CMP_STRIDE + CMP_LEN - 1
    cmask = blk_end[None, :] <= t_idx[:, None]
    s_c = jnp.einsum('bshd,bcd->bshc', q, kc).astype(jnp.float32) * scale
    p_c = masked_softmax(s_c, cmask[None, :, None, :]) * jnp.any(cmask, axis=-1)[None, :, None, None]
    o_cmp = jnp.einsum('bshc,bcd->bshd', p_c.astype(vc.dtype), vc)

    ns = S // SEL_LEN
    p_sel = jnp.einsum('bshc,cn->bsn', p_c, cmp_to_sel_matrix(S))
    blk = jnp.arange(ns)[None, :]
    cur = (t_idx // SEL_LEN)[:, None]
    valid = blk <= cur
    forced = (blk == 0) | (blk == cur) | (blk == cur - 1)
    p_sel = jnp.where(forced[None], FORCE_SCORE, jnp.where(valid[None], p_sel, -1.0))
    n_sel = min(N_SELECT, ns)
    _, sel_idx = lax.top_k(p_sel, n_sel)

    ks_blk = ks.reshape(B, ns, SEL_LEN, dh)
    vs_blk = vs.reshape(B, ns, SEL_LEN, dh)
    qb = q.reshape(B, nq, Q_BLOCK, H, dh).transpose(1, 0, 2, 3, 4)
    ib = sel_idx.reshape(B, nq, Q_BLOCK, n_sel).transpose(1, 0, 2, 3)
    b_ar = jnp.arange(B)[:, None, None]

    def sel_block(args):
        qi, idx, i = args
        kg = ks_blk[b_ar, idx]
        vg = vs_blk[b_ar, idx]
        key_pos = idx[..., None] * SEL_LEN + jnp.arange(SEL_LEN)
        q_pos = i * Q_BLOCK + jnp.arange(Q_BLOCK)
        mask = (key_pos <= q_pos[None, :, None, None]).reshape(B, Q_BLOCK, 1, n_sel * SEL_LEN)
        s = jnp.einsum('bqhd,bqnld->bqhnl', qi, kg).astype(jnp.float32) * scale
        p = masked_softmax(s.reshape(B, Q_BLOCK, H, n_sel * SEL_LEN), mask)
        p = p.reshape(B, Q_BLOCK, H, n_sel, SEL_LEN)
        return jnp.einsum('bqhnl,bqnld->bqhd', p.astype(vg.dtype), vg)

    o_sel = lax.map(sel_block, (qb, ib, jnp.arange(nq)))
    o_sel = o_sel.transpose(1, 0, 2, 3, 4).reshape(B, S, H, dh)

    n_band = WINDOW // Q_BLOCK + 1

    def band(z):
        zc = z.reshape(B, nq, Q_BLOCK, dh)
        zp = jnp.pad(zc, ((0, 0), (n_band - 1, 0), (0, 0), (0, 0)))
        return jnp.concatenate([zp[:, j:j + nq] for j in range(n_band)], axis=2)

    kb, vb = band(kw), band(vw)
    qc = q.reshape(B, nq, Q_BLOCK, H, dh)
    q_pos = jnp.arange(nq)[:, None] * Q_BLOCK + jnp.arange(Q_BLOCK)
    k_pos = (jnp.arange(nq)[:, None] - (n_band - 1)) * Q_BLOCK + jnp.arange(n_band * Q_BLOCK)
    diff = q_pos[:, :, None] - k_pos[:, None, :]
    wmask = (diff >= 0) & (diff < WINDOW) & (k_pos[:, None, :] >= 0)
    s_w = jnp.einsum('bnqhd,bnkd->bnhqk', qc, kb).astype(jnp.float32) * scale
    p_w = masked_softmax(s_w, wmask[None, :, None])
    o_win = jnp.einsum('bnhqk,bnkd->bnqhd', p_w.astype(vb.dtype), vb).reshape(B, S, H, dh)

    g = jax.nn.sigmoid(g_logits.astype(jnp.float32)).reshape(B, S, H, 3).astype(q.dtype)
    o = g[..., 0:1] * o_cmp + g[..., 1:2] * o_sel + g[..., 2:3] * o_win
    return o.reshape(B, S, H * dh)


def swiglu(x, wg, wu, wd):
    return (jax.nn.silu(x @ wg) * (x @ wu)) @ wd


def moe_ffn(x, w_router, wg, wu, wd):
    B, S, D = x.shape
    xt = x.reshape(B * S, D)
    logits = (xt @ w_router).astype(jnp.float32)
    top_v, top_i = lax.top_k(logits, TOP_K)
    w = jax.nn.softmax(top_v, axis=-1)
    gates = jnp.sum(jax.nn.one_hot(top_i, N_EXPERTS, dtype=jnp.float32) * w[..., None], axis=1)
    y = jnp.zeros_like(xt)
    for e in range(N_EXPERTS):
        y = y + gates[:, e:e + 1].astype(xt.dtype) * swiglu(xt, wg[e], wu[e], wd[e])
    return y.reshape(B, S, D)


def setup_inputs(seed: int = 0) -> dict:
    key = jax.random.key(seed)
    keys = jax.random.split(key, 32)

    def nrm(i, shape, scale):
        return jax.random.normal(keys[i], shape, jnp.float32) * scale

    def gain(i, shape):
        return 1.0 + nrm(i, shape, 0.02)

    nd, nm = (DEPTH + 1) // 2, DEPTH // 2
    L = DEPTH
    start = jax.random.randint(keys[1], (BATCH, 1), 0, MAX_START, dtype=jnp.int32)
    positions = start + jnp.arange(SEQ, dtype=jnp.int32)[None, :]
    return {
        'x': nrm(0, (BATCH, SEQ, D_MODEL), 1.0),
        'positions': positions,
        'attn_norm': gain(2, (L, D_MODEL)),
        'w_in': nrm(3, (L, D_MODEL, D_IN), D_MODEL ** -0.5),
        'mla_q_norm': gain(4, (L, MLA_Q_RANK)),
        'mla_w_uq': nrm(5, (L, MLA_Q_RANK, MLA_HEADS * (MLA_NOPE + MLA_ROPE)), MLA_Q_RANK ** -0.5),
        'mla_kv_norm': gain(6, (L, MLA_KV_RANK)),
        'mla_w_ukv': nrm(7, (L, MLA_KV_RANK, MLA_HEADS * (MLA_NOPE + MLA_V)), MLA_KV_RANK ** -0.5),
        'conv_w': nrm(8, (L, CONV_K, CONV_CH), CONV_K ** -0.5),
        'gmlp_ln': gain(9, (L, GROUP_W)),
        'gmlp_w_s': nrm(10, (L, GMLP_GROUPS, GMLP_CHUNK, GMLP_CHUNK), GMLP_CHUNK ** -0.5),
        'gmlp_b_s': gain(11, (L, GMLP_GROUPS, GMLP_CHUNK)),
        'nsa_pe_k': nrm(12, (L, CMP_LEN, NSA_DH), 0.1),
        'nsa_w1_k': nrm(13, (L, CMP_LEN * NSA_DH, CMP_HIDDEN), (CMP_LEN * NSA_DH) ** -0.5),
        'nsa_w2_k': nrm(14, (L, CMP_HIDDEN, NSA_DH), CMP_HIDDEN ** -0.5),
        'nsa_pe_v': nrm(15, (L, CMP_LEN, NSA_DH), 0.1),
        'nsa_w1_v': nrm(16, (L, CMP_LEN * NSA_DH, CMP_HIDDEN), (CMP_LEN * NSA_DH) ** -0.5),
        'nsa_w2_v': nrm(17, (L, CMP_HIDDEN, NSA_DH), CMP_HIDDEN ** -0.5),
        'group_norm': gain(18, (L, N_MIXERS, GROUP_W)),
        'w_o': nrm(19, (L, D_MIX, D_MODEL), D_MIX ** -0.5),
        'ffn_norm': gain(20, (L, D_MODEL)),
        'dense_w_gate': nrm(21, (nd, D_MODEL, D_FF), D_MODEL ** -0.5),
        'dense_w_up': nrm(22, (nd, D_MODEL, D_FF), D_MODEL ** -0.5),
        'dense_w_down': nrm(23, (nd, D_FF, D_MODEL), D_FF ** -0.5),
        'moe_router': nrm(24, (nm, D_MODEL, N_EXPERTS), D_MODEL ** -0.5),
        'moe_w_gate': nrm(25, (nm, N_EXPERTS, D_MODEL, D_FF), D_MODEL ** -0.5),
        'moe_w_up': nrm(26, (nm, N_EXPERTS, D_MODEL, D_FF), D_MODEL ** -0.5),
        'moe_w_down': nrm(27, (nm, N_EXPERTS, D_FF, D_MODEL), D_FF ** -0.5),
        'final_norm': gain(28, (D_MODEL,)),
    }


def reference(x, positions, attn_norm, w_in, mla_q_norm, mla_w_uq, mla_kv_norm, mla_w_ukv,
              conv_w, gmlp_ln, gmlp_w_s, gmlp_b_s,
              nsa_pe_k, nsa_w1_k, nsa_w2_k, nsa_pe_v, nsa_w1_v, nsa_w2_v,
              group_norm, w_o, ffn_norm,
              dense_w_gate, dense_w_up, dense_w_down,
              moe_router, moe_w_gate, moe_w_up, moe_w_down, final_norm):
    B, S, D = x.shape
    splits = np.cumsum(IN_SIZES)[:-1].tolist()
    h = x
    for l in range(DEPTH):
        xn = rms_norm(h, attn_norm[l])
        (cq, ckv, kr, cb, cc, chv, gu, gv,
         nq_, nkc, nvc, nks, nvs, nkw, nvw, ng) = jnp.split(xn @ w_in[l], splits, axis=-1)
        y_mla = mla_mixer(cq, ckv, kr, positions, mla_q_norm[l], mla_w_uq[l],
                          mla_kv_norm[l], mla_w_ukv[l])
        y_conv = short_conv_mixer(cb, cc, chv, conv_w[l])
        y_gmlp = gmlp_mixer(gu, gv, gmlp_ln[l], gmlp_w_s[l], gmlp_b_s[l])
        y_nsa = nsa_mixer(nq_, nkc, nvc, nks, nvs, nkw, nvw, ng,
                          nsa_pe_k[l], nsa_w1_k[l], nsa_w2_k[l],
                          nsa_pe_v[l], nsa_w1_v[l], nsa_w2_v[l])
        y = jnp.stack([y_mla, y_conv, y_gmlp, y_nsa], axis=2)
        y = rms_norm(y, group_norm[l]).reshape(B, S, D_MIX)
        h = h + y @ w_o[l]
        hn = rms_norm(h, ffn_norm[l])
        if l % 2 == 0:
            h = h + swiglu(hn, dense_w_gate[l // 2], dense_w_up[l // 2], dense_w_down[l // 2])
        else:
            h = h + moe_ffn(hn, moe_router[l // 2], moe_w_gate[l // 2], moe_w_up[l // 2],
                            moe_w_down[l // 2])
    return rms_norm(h, final_norm)
```

```python
import functools

import numpy as np
import jax
import jax.numpy as jnp
from jax import lax
from jax.experimental import pallas as pl
from jax.experimental.pallas import tpu as pltpu

F32 = jnp.float32
BF16 = jnp.bfloat16

D_MODEL = 1024
GROUP_W = 256
MLA_HEADS = 4
MLA_NOPE = 64
MLA_ROPE = 32
MLA_V = 64
MLA_Q_RANK = 256
MLA_KV_RANK = 128
ROPE_THETA = 10000.0
CONV_K = 3
GMLP_GROUPS = 4
GMLP_CHUNK = 128
NSA_HEADS = 4
NSA_DH = 64
CMP_LEN = 32
CMP_STRIDE = 16
CMP_HIDDEN = 256
SEL_LEN = 64
N_SELECT = 16
WINDOW = 512
FORCE_SCORE = 1e4
D_FF = 3584
N_EXPERTS = 8
EPS = 1e-6
NEG = -1e30

LANES = 128
VMEM_LIMIT = 56 * 1024 * 1024

_IN_SIZES = (256, 128, 32, 256, 256, 256, 256, 256, 256, 64, 64, 64, 64, 64, 64, 12)
_IN_OFF = np.concatenate([[0], np.cumsum(_IN_SIZES)]).tolist()

_W_MLA, _W_CONV, _W_GMLP, _W_NQ, _W_KV, _W_GATE = 640, 768, 512, 256, 128, 128
_IN_SPLITS = (_W_MLA, _W_CONV, _W_GMLP, _W_NQ, _W_KV, _W_KV, _W_KV, _W_GATE)
_D_IN_P = sum(_IN_SPLITS)

_NT = (((1,), (1,)), ((), ()))


def _cparams(sem, vmem=VMEM_LIMIT):
    return pltpu.CompilerParams(dimension_semantics=sem, vmem_limit_bytes=vmem)


def _rms(x, g):
    return x * lax.rsqrt(jnp.mean(x * x, axis=-1, keepdims=True) + EPS) * g


def _dot(a, b):
    return jnp.dot(a, b, preferred_element_type=F32)


def _dot_exact(a, b):
    return jnp.dot(a, b, preferred_element_type=F32, precision=lax.Precision.HIGHEST)


def _gelu(x):
    return 0.5 * x * (1.0 + jnp.tanh(0.7978845608028654 * (x + 0.044715 * (x * x * x))))


def _in_proj_kernel(h_ref, g_ref, w_ref, *o_refs):
    xn = _rms(h_ref[...], g_ref[...]).astype(BF16)
    r = _dot(xn, w_ref[...]).astype(BF16)
    off = 0
    for o_ref, w in zip(o_refs, _IN_SPLITS):
        o_ref[...] = r[:, off:off + w]
        off += w


def in_proj(h, g, w):
    T, D = h.shape
    tm = min(512, T)
    return pl.pallas_call(
        _in_proj_kernel, grid=(T // tm,),
        in_specs=[pl.BlockSpec((tm, D), lambda i: (i, 0)),
                  pl.BlockSpec((1, D), lambda i: (0, 0)),
                  pl.BlockSpec((D, _D_IN_P), lambda i: (0, 0))],
        out_specs=[pl.BlockSpec((tm, w_), lambda i: (i, 0)) for w_ in _IN_SPLITS],
        out_shape=[jax.ShapeDtypeStruct((T, w_), BF16) for w_ in _IN_SPLITS],
        compiler_params=_cparams(("parallel",)), name="in_proj")(h, g, w)


def prep_w_in(w_in):
    o = _IN_OFF
    L, D, _ = w_in.shape
    z = lambda n: jnp.zeros((L, D, n), F32)
    kr = w_in[:, :, o[2]:o[3]]
    half = MLA_ROPE // 2
    kr_sw = jnp.concatenate([kr[..., half:], kr[..., :half]], axis=-1)
    cols = [w_in[:, :, o[0]:o[2]],
            z(64), kr, z(32), z(64), kr_sw, z(32),
            w_in[:, :, o[3]:o[6]],
            w_in[:, :, o[6]:o[8]],
            w_in[:, :, o[8]:o[9]] * (NSA_DH ** -0.5),
            w_in[:, :, o[9]:o[15]],
            w_in[:, :, o[15]:o[16]], z(_W_GATE - 12)]
    return jnp.concatenate(cols, axis=-1).astype(BF16)


def _mla_proj_kernel(x_ref, pos_ref, qg_ref, kvg_ref, wqa_ref, wqb_ref, wkv_ref, freq_ref, sign_ref,
                     q_ref, k_ref, v_ref):
    x = x_ref[...].astype(F32)
    cq, ckv = x[:, :256], x[:, 256:384]
    kra, krb = x[:, 384:512], x[:, 512:640]
    ang = pos_ref[...] * freq_ref[...]
    c = jnp.cos(ang)
    s = jnp.sin(ang) * sign_ref[...]
    c4 = jnp.concatenate([c] * MLA_HEADS, axis=1)
    s4 = jnp.concatenate([s] * MLA_HEADS, axis=1)
    cqn = _rms(cq, qg_ref[...]).astype(BF16)
    q = _dot(cqn, wqa_ref[...]) * c4 + _dot(cqn, wqb_ref[...]) * s4
    q_ref[...] = q.astype(BF16)
    ckvn = _rms(ckv, kvg_ref[...]).astype(BF16)
    kv = _dot(ckvn, wkv_ref[...])
    krot = kra * c + krb * s
    k = kv[:, :512] + jnp.concatenate([krot] * MLA_HEADS, axis=1)
    k_ref[...] = k.astype(BF16)
    v_ref[...] = kv[:, 512:].astype(BF16)


def mla_proj(x_mla, pos, qg, kvg, wqa, wqb, wkv, freq, sign):
    T = x_mla.shape[0]
    tm = min(512, T)
    full = lambda a: pl.BlockSpec(a.shape, lambda i: (0,) * a.ndim)
    return pl.pallas_call(
        _mla_proj_kernel, grid=(T // tm,),
        in_specs=[pl.BlockSpec((tm, _W_MLA), lambda i: (i, 0)),
                  pl.BlockSpec((tm, 1), lambda i: (i, 0)),
                  full(qg), full(kvg), full(wqa), full(wqb), full(wkv), full(freq), full(sign)],
        out_specs=[pl.BlockSpec((tm, 512), lambda i: (i, 0)),
                   pl.BlockSpec((tm, 512), lambda i: (i, 0)),
                   pl.BlockSpec((tm, 256), lambda i: (i, 0))],
        out_shape=[jax.ShapeDtypeStruct((T, 512), BF16),
                   jax.ShapeDtypeStruct((T, 512), BF16),
                   jax.ShapeDtypeStruct((T, 256), BF16)],
        compiler_params=_cparams(("parallel",)), name="mla_proj")(
            x_mla, pos, qg, kvg, wqa, wqb, wkv, freq, sign)


def prep_mla(w_uq, w_ukv):
    L = w_uq.shape[0]
    scale = (MLA_NOPE + MLA_ROPE) ** -0.5
    half = MLA_ROPE // 2
    hq = MLA_NOPE + MLA_ROPE
    a, b, kn, vv = [], [], [], []
    for h in range(MLA_HEADS):
        nope = w_uq[:, :, h * hq:h * hq + MLA_NOPE]
        rp = w_uq[:, :, h * hq + MLA_NOPE:(h + 1) * hq]
        rp_sw = jnp.concatenate([rp[..., half:], rp[..., :half]], axis=-1)
        a += [nope, rp, jnp.zeros((L, MLA_Q_RANK, 32), F32)]
        b += [jnp.zeros((L, MLA_Q_RANK, 64), F32), rp_sw, jnp.zeros((L, MLA_Q_RANK, 32), F32)]
        hk = MLA_NOPE + MLA_V
        kn += [w_ukv[:, :, h * hk:h * hk + MLA_NOPE], jnp.zeros((L, MLA_KV_RANK, 64), F32)]
        vv += [w_ukv[:, :, h * hk + MLA_NOPE:(h + 1) * hk]]
    wqa = (jnp.concatenate(a, axis=-1) * scale).astype(BF16)
    wqb = (jnp.concatenate(b, axis=-1) * scale).astype(BF16)
    wkv = jnp.concatenate(kn + vv, axis=-1).astype(BF16)
    inv = ROPE_THETA ** (-jnp.arange(half, dtype=F32) / half)
    freq = jnp.concatenate([jnp.zeros((64,), F32), inv, inv, jnp.zeros((32,), F32)])[None, :]
    sign = jnp.concatenate([jnp.zeros((64,), F32), -jnp.ones((half,), F32), jnp.ones((half,), F32),
                            jnp.zeros((32,), F32)])[None, :]
    return wqa, wqb, wkv, freq, sign


def _flash_update(s, v, m_ref, l_ref, acc_ref, idx):
    m_prev = m_ref[idx]
    m_new = jnp.maximum(m_prev, jnp.max(s, axis=-1, keepdims=True))
    alpha = jnp.exp(m_prev - m_new)
    p = jnp.exp(s - m_new)
    l_ref[idx] = alpha * l_ref[idx] + jnp.sum(p, axis=-1, keepdims=True)
    acc_ref[idx] = alpha * acc_ref[idx] + _dot(p.astype(BF16), v)
    m_ref[idx] = m_new


def _causal_mask(t):
    row = lax.broadcasted_iota(jnp.int32, (t, t), 0)
    col = lax.broadcasted_iota(jnp.int32, (t, t), 1)
    return row >= col


def _mla_attn_kernel(q_ref, k_ref, v_ref, o_ref, m_ref, l_ref, acc_ref, *, t):
    i = pl.program_id(1)
    m_ref[...] = jnp.full(m_ref.shape, NEG, F32)
    l_ref[...] = jnp.zeros(l_ref.shape, F32)
    acc_ref[...] = jnp.zeros(acc_ref.shape, F32)

    def tile(j, masked):
        for h in range(MLA_HEADS):
            qh = q_ref[:, h * LANES:(h + 1) * LANES]
            kt = k_ref[pl.ds(j * t, t), h * LANES:(h + 1) * LANES]
            vt = v_ref[pl.ds(j * t, t), (h // 2) * LANES:(h // 2 + 1) * LANES]
            s = lax.dot_general(qh, kt, _NT, preferred_element_type=F32)
            if masked:
                s = jnp.where(_causal_mask(t), s, NEG)
            _flash_update(s, vt, m_ref, l_ref, acc_ref, h)

    def body(j, c):
        tile(j, False)
        return c

    lax.fori_loop(0, i, body, 0)
    tile(i, True)
    lane = lax.broadcasted_iota(jnp.int32, (t, LANES), 1)
    outs = [acc_ref[h] / l_ref[h] for h in range(MLA_HEADS)]
    pairs = [jnp.where(lane < MLA_V, outs[2 * p], outs[2 * p + 1]) for p in range(MLA_HEADS // 2)]
    o_ref[...] = jnp.concatenate(pairs, axis=1).astype(BF16)


def mla_attn(q, k, v, B, S):
    t = min(256, S)
    q3, k3, v3 = q.reshape(B, S, 512), k.reshape(B, S, 512), v.reshape(B, S, 256)
    out = pl.pallas_call(
        functools.partial(_mla_attn_kernel, t=t), grid=(B, S // t),
        in_specs=[pl.BlockSpec((None, t, 512), lambda b, i: (b, i, 0)),
                  pl.BlockSpec((None, S, 512), lambda b, i: (b, 0, 0)),
                  pl.BlockSpec((None, S, 256), lambda b, i: (b, 0, 0))],
        out_specs=pl.BlockSpec((None, t, 256), lambda b, i: (b, i, 0)),
        out_shape=jax.ShapeDtypeStruct((B, S, 256), BF16),
        scratch_shapes=[pltpu.VMEM((MLA_HEADS, t, 1), F32), pltpu.VMEM((MLA_HEADS, t, 1), F32),
                        pltpu.VMEM((MLA_HEADS, t, LANES), F32)],
        compiler_params=_cparams(("parallel", "arbitrary")), name="mla_attn")(q3, k3, v3)
    return out.reshape(B * S, 256)


def _conv_gmlp_kernel(xc_ref, halo_ref, xg_ref, cw_ref, ln_ref, ws_ref, bs_ref, o_ref, *, tiles_per_seq):
    i = pl.program_id(0)
    tm = xc_ref.shape[0]
    xc = xc_ref[...].astype(F32)
    cb, u = xc[:, :256], xc[:, 256:512] * xc[:, 512:768]
    hx = halo_ref[...].astype(F32)
    keep = jnp.where(i % tiles_per_seq == 0, 0.0, 1.0)
    hu = hx[:, 256:512] * hx[:, 512:768] * keep
    p1, p2 = hu[7:8, :], hu[6:7, :]
    row = lax.broadcasted_iota(jnp.int32, (tm, 256), 0)
    u1 = jnp.where(row == 0, p1, pltpu.roll(u, 1, axis=0))
    u2 = jnp.where(row == 0, p2, jnp.where(row == 1, p1, pltpu.roll(u, 2, axis=0)))
    cw = cw_ref[...]
    y_conv = cb * (cw[0:1, :] * u2 + cw[1:2, :] * u1 + cw[2:3, :] * u)
    o_ref[:, :256] = y_conv.astype(BF16)

    xg = xg_ref[...].astype(F32)
    gu = _gelu(xg[:, :256])
    gv = _gelu(xg[:, 256:512])
    mu = jnp.mean(gv, axis=-1, keepdims=True)
    var = jnp.mean(jnp.square(gv - mu), axis=-1, keepdims=True)
    vn = (gv - mu) * lax.rsqrt(var + EPS) * ln_ref[...]
    lane_g = lax.broadcasted_iota(jnp.int32, (GMLP_CHUNK, 256), 1) // (256 // GMLP_GROUPS)
    tril = _causal_mask(GMLP_CHUNK)
    for c in range(tm // GMLP_CHUNK):
        vc = vn[c * GMLP_CHUNK:(c + 1) * GMLP_CHUNK, :]
        sv = bs_ref[...]
        for g in range(GMLP_GROUPS):
            wg = jnp.where(tril, ws_ref[g], 0.0).astype(BF16)
            sv = sv + _dot(wg, jnp.where(lane_g == g, vc, 0.0).astype(BF16))
        o_ref[c * GMLP_CHUNK:(c + 1) * GMLP_CHUNK, 256:512] = (
            gu[c * GMLP_CHUNK:(c + 1) * GMLP_CHUNK, :] * sv).astype(BF16)


def conv_gmlp(x_conv, x_gmlp, conv_w8, ln_g, w_s, b_exp, S):
    T = x_conv.shape[0]
    tm = min(512, S)
    full = lambda a: pl.BlockSpec(a.shape, lambda i: (0,) * a.ndim)
    return pl.pallas_call(
        functools.partial(_conv_gmlp_kernel, tiles_per_seq=S // tm), grid=(T // tm,),
        in_specs=[pl.BlockSpec((tm, _W_CONV), lambda i: (i, 0)),
                  pl.BlockSpec((8, _W_CONV), lambda i: (jnp.maximum(i * (tm // 8) - 1, 0), 0)),
                  pl.BlockSpec((tm, _W_GMLP), lambda i: (i, 0)),
                  full(conv_w8), full(ln_g), full(w_s), full(b_exp)],
        out_specs=pl.BlockSpec((tm, 512), lambda i: (i, 0)),
        out_shape=jax.ShapeDtypeStruct((T, 512), BF16),
        compiler_params=_cparams(("parallel",)), name="conv_gmlp")(
            x_conv, x_conv, x_gmlp, conv_w8, ln_g, w_s, b_exp)


def _compress_kernel(z_ref, pea_ref, peb_ref, w1a_ref, w1b_ref, w2_ref, o_ref):
    z = z_ref[...]
    n = z.shape[0]
    a = _dot(z, w1a_ref[...])
    b = _dot(z, w1b_ref[...])
    c = _dot_exact(pea_ref[...], w1a_ref[...].astype(F32)) + _dot_exact(peb_ref[...], w1b_ref[...].astype(F32))
    pre = a + pltpu.roll(b, n - 1, axis=0) + c[0:1, :]
    o_ref[...] = _dot(_gelu(pre).astype(BF16), w2_ref[...]).astype(BF16)


def compress(z2, pea, peb, w1a, w1b, w2):
    B, n, K = z2.shape
    full = lambda a: pl.BlockSpec(a.shape, lambda b: (0,) * a.ndim)
    return pl.pallas_call(
        _compress_kernel, grid=(B,),
        in_specs=[pl.BlockSpec((None, n, K), lambda b: (b, 0, 0)),
                  full(pea), full(peb), full(w1a), full(w1b), full(w2)],
        out_specs=pl.BlockSpec((None, n, LANES), lambda b: (b, 0, 0)),
        out_shape=jax.ShapeDtypeStruct((B, n, LANES), BF16),
        compiler_params=_cparams(("parallel",)), name="nsa_compress")(z2, pea, peb, w1a, w1b, w2)


def prep_compress(pe_k, w1_k, w2_k, pe_v, w1_v, w2_v):
    half = CMP_LEN // 2

    def w1_half(w1k, w1v):
        wk = w1k.reshape(half, NSA_DH, CMP_HIDDEN)
        wv = w1v.reshape(half, NSA_DH, CMP_HIDDEN)
        zk = jnp.zeros_like(wk)
        top = jnp.concatenate([wk, zk], axis=-1)
        bot = jnp.concatenate([zk, wv], axis=-1)
        return jnp.concatenate([top, bot], axis=1).reshape(half * 2 * NSA_DH, 2 * CMP_HIDDEN)

    n1 = half * NSA_DH
    w1a = w1_half(w1_k[:n1], w1_v[:n1]).astype(BF16)
    w1b = w1_half(w1_k[n1:], w1_v[n1:]).astype(BF16)

    def pe_half(pk, pv):
        row = jnp.concatenate([pk, pv], axis=-1).reshape(1, half * 2 * NSA_DH)
        return jnp.broadcast_to(row, (8, half * 2 * NSA_DH))

    pea = pe_half(pe_k[:half], pe_v[:half])
    peb = pe_half(pe_k[half:], pe_v[half:])
    z = jnp.zeros((CMP_HIDDEN, NSA_DH), F32)
    w2 = jnp.concatenate([jnp.concatenate([w2_k, z], axis=1),
                          jnp.concatenate([z, w2_v], axis=1)], axis=0).astype(BF16)
    return pea, peb, w1a, w1b, w2


def _head_select(nh):
    sel = np.zeros((nh, nh * NSA_DH, LANES), np.float32)
    place = np.zeros((nh, LANES, nh * NSA_DH), np.float32)
    for h in range(nh):
        for d in range(NSA_DH):
            sel[h, h * NSA_DH + d, d] = 1.0
            place[h, NSA_DH + d, h * NSA_DH + d] = 1.0
    return jnp.asarray(sel, BF16), jnp.asarray(place, BF16)


def _stack_heads(q, sel_ref, qs_ref):
    for h in range(NSA_HEADS):
        qs_ref[h] = _dot(q, sel_ref[h]).astype(BF16)


def _place_heads(o_list, place_ref):
    out = _dot(o_list[0].astype(BF16), place_ref[0])
    for h in range(1, NSA_HEADS):
        out = out + _dot(o_list[h].astype(BF16), place_ref[h])
    return out


def _nsa_cmp_kernel(q_ref, kv_ref, msel_ref, sel_ref, place_ref, o_ref, bias_ref, qs_ref, *, t, ns):
    i = pl.program_id(1)
    nc = kv_ref.shape[0]
    _stack_heads(q_ref[...], sel_ref, qs_ref)
    kv = kv_ref[...]
    tpos = i * t + lax.broadcasted_iota(jnp.int32, (t, nc), 0)
    blk_end = lax.broadcasted_iota(jnp.int32, (t, nc), 1) * CMP_STRIDE + (CMP_LEN - 1)
    cmask = blk_end <= tpos
    p_sum = jnp.zeros((t, nc), F32)
    outs = []
    for h in range(NSA_HEADS):
        s = lax.dot_general(qs_ref[h], kv, _NT, preferred_element_type=F32)
        s = jnp.where(cmask, s, NEG)
        m = jnp.max(s, axis=-1, keepdims=True)
        e = jnp.where(cmask, jnp.exp(s - m), 0.0)
        l = jnp.sum(e, axis=-1, keepdims=True)
        p = e / jnp.where(l > 0.0, l, 1.0)
        p_sum = p_sum + p
        outs.append(_dot(p.astype(BF16), kv))
    o_ref[...] = _place_heads(outs, place_ref).astype(BF16)

    p_sel = _dot_exact(p_sum, msel_ref[...])
    lane = lax.broadcasted_iota(jnp.int32, (t, LANES), 1)
    lane_f = lane.astype(F32)
    cur = (i * t + lax.broadcasted_iota(jnp.int32, (t, LANES), 0)) // SEL_LEN
    forced = (lane == 0) | (lane == cur) | (lane == cur - 1)
    v = jnp.where(forced, FORCE_SCORE, jnp.where(lane <= cur, p_sel, -1.0))
    v = jnp.where(lane < ns, v, -jnp.inf)
    chosen = jnp.zeros((t, LANES), jnp.bool_)
    for _ in range(min(N_SELECT, ns)):
        mx = jnp.max(v, axis=-1, keepdims=True)
        idx = jnp.min(jnp.where(v == mx, lane_f, float(LANES)), axis=-1, keepdims=True)
        hit = lane_f == idx
        chosen = chosen | hit
        v = jnp.where(hit, -jnp.inf, v)
    bias_ref[...] = jnp.where(chosen, 0.0, NEG).astype(BF16)


def nsa_cmp(q, kcv, msel, sel, place, B, S):
    t = min(256, S)
    nc = kcv.shape[1]
    ns = S // SEL_LEN
    full = lambda a: pl.BlockSpec(a.shape, lambda b, i: (0,) * a.ndim)
    return pl.pallas_call(
        functools.partial(_nsa_cmp_kernel, t=t, ns=ns), grid=(B, S // t),
        in_specs=[pl.BlockSpec((None, t, 256), lambda b, i: (b, i, 0)),
                  pl.BlockSpec((None, nc, LANES), lambda b, i: (b, 0, 0)),
                  full(msel), full(sel), full(place)],
        out_specs=[pl.BlockSpec((None, t, 256), lambda b, i: (b, i, 0)),
                   pl.BlockSpec((None, t, LANES), lambda b, i: (b, i, 0))],
        out_shape=[jax.ShapeDtypeStruct((B, S, 256), BF16), jax.ShapeDtypeStruct((B, S, LANES), BF16)],
        scratch_shapes=[pltpu.VMEM((NSA_HEADS, t, LANES), BF16)],
        compiler_params=_cparams(("parallel", "arbitrary")), name="nsa_cmp")(
            q.reshape(B, S, 256), kcv, msel, sel, place)


def cmp_to_sel(S):
    nc = S // CMP_STRIDE - CMP_LEN // CMP_STRIDE + 1
    ns = S // SEL_LEN
    cs = np.arange(nc)[:, None] * CMP_STRIDE
    ss = np.arange(ns)[None, :] * SEL_LEN
    ov = np.clip(np.minimum(cs + CMP_LEN, ss + SEL_LEN) - np.maximum(cs, ss), 0, None)
    m = np.zeros((S // CMP_STRIDE, LANES), np.float32)
    m[:nc, :ns] = ov.astype(np.float32) / np.float32(CMP_LEN)
    return jnp.asarray(m)


def _nsa_attn_kernel(q_ref, ksel_ref, kwin_ref, bias_ref, ocmp_ref, gl_ref, sel_ref, place_ref, gx_ref,
                     o_ref, qs_ref, m_ref, l_ref, acc_ref, *, t):
    i = pl.program_id(1)
    _stack_heads(q_ref[...], sel_ref, qs_ref)
    row = lax.broadcasted_iota(jnp.int32, (t, t), 0)
    col = lax.broadcasted_iota(jnp.int32, (t, t), 1)

    def reset():
        m_ref[...] = jnp.full(m_ref.shape, NEG, F32)
        l_ref[...] = jnp.zeros(l_ref.shape, F32)
        acc_ref[...] = jnp.zeros(acc_ref.shape, F32)

    def tile(kv, bias, mask):
        for h in range(NSA_HEADS):
            s = lax.dot_general(qs_ref[h], kv, _NT, preferred_element_type=F32)
            if bias is not None:
                s = s + bias
            if mask is not None:
                s = jnp.where(mask, s, NEG)
            _flash_update(s, kv, m_ref, l_ref, acc_ref, h)

    def result():
        return _place_heads([acc_ref[h] / l_ref[h] for h in range(NSA_HEADS)], place_ref)

    sel_bias = bias_ref[...]
    blk_r = lax.broadcasted_iota(jnp.int32, (LANES, t), 0)
    blk_c = lax.broadcasted_iota(jnp.int32, (LANES, t), 1)

    def expand(j):
        e = jnp.where(blk_r == (j * t + blk_c) // SEL_LEN, 1.0, 0.0).astype(BF16)
        return _dot(sel_bias, e)

    reset()

    def sel_body(j, c):
        tile(ksel_ref[pl.ds(j * t, t), :], expand(j), None)
        return c

    lax.fori_loop(0, i, sel_body, 0)
    tile(ksel_ref[pl.ds(i * t, t), :], expand(i), row >= col)
    o_sel = result()

    reset()
    nback = WINDOW // t

    @pl.when(i >= nback)
    def _():
        tile(kwin_ref[pl.ds((i - nback) * t, t), :], None, row + nback * t - col < WINDOW)

    def win_body(j, c):
        tile(kwin_ref[pl.ds(j * t, t), :], None, None)
        return c

    lax.fori_loop(jnp.maximum(i - nback + 1, 0), i, win_body, 0)
    tile(kwin_ref[pl.ds(i * t, t), :], None, row >= col)
    o_win = result()

    g = 1.0 / (1.0 + jnp.exp(-gl_ref[...].astype(F32)))
    o = (_dot_exact(g, gx_ref[0]) * ocmp_ref[...].astype(F32)
         + _dot_exact(g, gx_ref[1]) * o_sel + _dot_exact(g, gx_ref[2]) * o_win)
    o_ref[...] = o.astype(BF16)


def nsa_attn(q, ksel, kwin, bias, o_cmp, gl, sel, place, gx, B, S):
    t = min(256, S)
    full = lambda a: pl.BlockSpec(a.shape, lambda b, i: (0,) * a.ndim)
    tile2 = lambda w: pl.BlockSpec((None, t, w), lambda b, i: (b, i, 0))
    seq = pl.BlockSpec((None, S, LANES), lambda b, i: (b, 0, 0))
    out = pl.pallas_call(
        functools.partial(_nsa_attn_kernel, t=t), grid=(B, S // t),
        in_specs=[tile2(256), seq, seq, tile2(LANES), tile2(256), tile2(LANES), full(sel), full(place), full(gx)],
        out_specs=tile2(256),
        out_shape=jax.ShapeDtypeStruct((B, S, 256), BF16),
        scratch_shapes=[pltpu.VMEM((NSA_HEADS, t, LANES), BF16),
                        pltpu.VMEM((NSA_HEADS, t, 1), F32), pltpu.VMEM((NSA_HEADS, t, 1), F32),
                        pltpu.VMEM((NSA_HEADS, t, LANES), F32)],
        compiler_params=_cparams(("parallel", "arbitrary")), name="nsa_attn")(
            q.reshape(B, S, 256), ksel.reshape(B, S, LANES), kwin.reshape(B, S, LANES), bias, o_cmp,
            gl.reshape(B, S, LANES), sel, place, gx)
    return out.reshape(B * S, 256)


def _gate_expand():
    gx = np.zeros((3, LANES, NSA_HEADS * NSA_DH), np.float32)
    for br in range(3):
        for h in range(NSA_HEADS):
            gx[br, h * 3 + br, h * NSA_DH:(h + 1) * NSA_DH] = 1.0
    return jnp.asarray(gx)


def _mix_out_kernel(ymla_ref, ycg_ref, ynsa_ref, h_ref, gn_ref, wo_ref, o_ref):
    gn = gn_ref[...]
    parts = [ymla_ref[...].astype(F32), ycg_ref[:, :256].astype(F32), ycg_ref[:, 256:].astype(F32),
             ynsa_ref[...].astype(F32)]
    yn = [_rms(p, gn[:, k * GROUP_W:(k + 1) * GROUP_W]).astype(BF16) for k, p in enumerate(parts)]
    o_ref[...] = h_ref[...] + _dot(jnp.concatenate(yn, axis=1), wo_ref[...])


def mix_out(y_mla, y_cg, y_nsa, h, gn, wo):
    T, D = h.shape
    tm = min(512, T)
    row = lambda w: pl.BlockSpec((tm, w), lambda i: (i, 0))
    return pl.pallas_call(
        _mix_out_kernel, grid=(T // tm,),
        in_specs=[row(256), row(512), row(256), row(D),
                  pl.BlockSpec((1, D), lambda i: (0, 0)), pl.BlockSpec((D, D), lambda i: (0, 0))],
        out_specs=row(D), out_shape=jax.ShapeDtypeStruct((T, D), F32),
        compiler_params=_cparams(("parallel",)), name="mix_out")(y_mla, y_cg, y_nsa, h, gn, wo)


def _ffn_step(j, nj, x_ref, g_ref, wg_ref, wu_ref, wd_ref, o_ref, hn_ref, acc_ref, residual):
    @pl.when(j == 0)
    def _():
        hn_ref[...] = _rms(x_ref[...], g_ref[...]).astype(BF16)
        acc_ref[...] = jnp.zeros(acc_ref.shape, F32)

    hn = hn_ref[...]
    a = _dot(hn, wg_ref[...])
    b = _dot(hn, wu_ref[...])
    act = (a / (1.0 + jnp.exp(-a)) * b).astype(BF16)
    acc_ref[...] += _dot(act, wd_ref[...])

    @pl.when(j == nj - 1)
    def _():
        o_ref[...] = (x_ref[...] + acc_ref[...]) if residual else acc_ref[...]


def _ffn_kernel(x_ref, g_ref, wg_ref, wu_ref, wd_ref, o_ref, hn_ref, acc_ref):
    _ffn_step(pl.program_id(1), pl.num_programs(1), x_ref, g_ref, wg_ref, wu_ref, wd_ref, o_ref, hn_ref,
              acc_ref, True)


_FFN_TF = 512


def ffn_dense(h, g, wg, wu, wd):
    T, D = h.shape
    F = wg.shape[1]
    tm = min(1024, T)
    return pl.pallas_call(
        _ffn_kernel, grid=(T // tm, F // _FFN_TF),
        in_specs=[pl.BlockSpec((tm, D), lambda i, j: (i, 0)),
                  pl.BlockSpec((1, D), lambda i, j: (0, 0)),
                  pl.BlockSpec((D, _FFN_TF), lambda i, j: (0, j)),
                  pl.BlockSpec((D, _FFN_TF), lambda i, j: (0, j)),
                  pl.BlockSpec((_FFN_TF, D), lambda i, j: (j, 0))],
        out_specs=pl.BlockSpec((tm, D), lambda i, j: (i, 0)),
        out_shape=jax.ShapeDtypeStruct((T, D), F32),
        scratch_shapes=[pltpu.VMEM((tm, D), BF16), pltpu.VMEM((tm, D), F32)],
        compiler_params=_cparams(("parallel", "arbitrary")), name="ffn_dense")(h, g, wg, wu, wd)


def _ffn_grouped_kernel(te_ref, nu_ref, x_ref, g_ref, wg_ref, wu_ref, wd_ref, o_ref, hn_ref, acc_ref):
    del te_ref
    i, j, nj = pl.program_id(0), pl.program_id(1), pl.num_programs(1)
    used = i < nu_ref[0]

    @pl.when(used)
    def _():
        _ffn_step(j, nj, x_ref, g_ref, wg_ref, wu_ref, wd_ref, o_ref, hn_ref, acc_ref, False)

    @pl.when(jnp.logical_not(used))
    def _():
        o_ref[...] = jnp.zeros(o_ref.shape, F32)


def ffn_grouped(xs, g, wg, wu, wd, tile_expert, n_used, tm):
    NP, D = xs.shape
    F = wg.shape[2]
    nf = F // _FFN_TF

    def jj(i, j, nu):
        return jnp.where(i < nu[0], j, nf - 1)

    grid_spec = pltpu.PrefetchScalarGridSpec(
        num_scalar_prefetch=2, grid=(NP // tm, nf),
        in_specs=[pl.BlockSpec((tm, D), lambda i, j, te, nu: (i, 0)),
                  pl.BlockSpec((1, D), lambda i, j, te, nu: (0, 0)),
                  pl.BlockSpec((None, D, _FFN_TF), lambda i, j, te, nu: (te[i], 0, jj(i, j, nu))),
                  pl.BlockSpec((None, D, _FFN_TF), lambda i, j, te, nu: (te[i], 0, jj(i, j, nu))),
                  pl.BlockSpec((None, _FFN_TF, D), lambda i, j, te, nu: (te[i], jj(i, j, nu), 0))],
        out_specs=pl.BlockSpec((tm, D), lambda i, j, te, nu: (i, 0)),
        scratch_shapes=[pltpu.VMEM((tm, D), BF16), pltpu.VMEM((tm, D), F32)])
    return pl.pallas_call(
        _ffn_grouped_kernel, grid_spec=grid_spec,
        out_shape=jax.ShapeDtypeStruct((NP, D), F32),
        compiler_params=_cparams(("arbitrary", "arbitrary")), name="ffn_grouped")(
            tile_expert, n_used, xs, g, wg, wu, wd)


def _router_kernel(h_ref, g_ref, wr_ref, o_ref):
    hn = _rms(h_ref[...], g_ref[...])
    logits = _dot_exact(hn, wr_ref[...])
    tm = logits.shape[0]
    lane = lax.broadcasted_iota(jnp.int32, (tm, LANES), 1)
    lane_f = lane.astype(F32)
    lg = jnp.where(lane < N_EXPERTS, logits, -jnp.inf)
    m1 = jnp.max(lg, axis=-1, keepdims=True)
    i1 = jnp.min(jnp.where(lg == m1, lane_f, float(LANES)), axis=-1, keepdims=True)
    lg2 = jnp.where(lane_f == i1, -jnp.inf, lg)
    m2 = jnp.max(lg2, axis=-1, keepdims=True)
    i2 = jnp.min(jnp.where(lg2 == m2, lane_f, float(LANES)), axis=-1, keepdims=True)
    e = jnp.exp(m2 - m1)
    w1 = 1.0 / (1.0 + e)
    w2 = e / (1.0 + e)
    o_ref[...] = jnp.where(lane == 0, i1,
                           jnp.where(lane == 1, i2,
                                     jnp.where(lane == 2, w1, jnp.where(lane == 3, w2, 0.0))))


def router(h, g, wr):
    T, D = h.shape
    tm = min(512, T)
    return pl.pallas_call(
        _router_kernel, grid=(T // tm,),
        in_specs=[pl.BlockSpec((tm, D), lambda i: (i, 0)), pl.BlockSpec((1, D), lambda i: (0, 0)),
                  pl.BlockSpec((D, LANES), lambda i: (0, 0))],
        out_specs=pl.BlockSpec((tm, LANES), lambda i: (i, 0)),
        out_shape=jax.ShapeDtypeStruct((T, LANES), F32),
        compiler_params=_cparams(("parallel",)), name="moe_router")(h, g, wr)


def _row_copy(src, s, dst, d, sem):
    return pltpu.make_async_copy(src.at[pl.ds(s, 1)], dst.at[pl.ds(d, 1)], sem)


def _dispatch_kernel(dest_ref, h_ref, xs_in_ref, xs_ref, sem):
    del xs_in_ref
    tm = h_ref.shape[0]

    def issue(r, c):
        _row_copy(h_ref, r, xs_ref, dest_ref[0, 0, 2 * r], sem).start()
        _row_copy(h_ref, r, xs_ref, dest_ref[0, 0, 2 * r + 1], sem).start()
        return c

    lax.fori_loop(0, tm, issue, 0)

    def drain(r, c):
        _row_copy(h_ref, 0, xs_ref, 0, sem).wait()
        _row_copy(h_ref, 0, xs_ref, 0, sem).wait()
        return c

    lax.fori_loop(0, tm, drain, 0)


def dispatch(h, dest, xs_zero):
    T, D = h.shape
    tm = min(512, T)
    dest3 = dest.reshape(T // tm, 1, 2 * tm)
    return pl.pallas_call(
        _dispatch_kernel, grid=(T // tm,),
        in_specs=[pl.BlockSpec((1, 1, 2 * tm), lambda i: (i, 0, 0), memory_space=pltpu.SMEM),
                  pl.BlockSpec((tm, D), lambda i: (i, 0)),
                  pl.BlockSpec(memory_space=pl.ANY)],
        out_specs=pl.BlockSpec(memory_space=pl.ANY),
        out_shape=jax.ShapeDtypeStruct(xs_zero.shape, F32),
        scratch_shapes=[pltpu.SemaphoreType.DMA(())],
        input_output_aliases={2: 0},
        compiler_params=_cparams(("arbitrary",)), name="moe_dispatch")(dest3, h, xs_zero)


def _combine_kernel(dest_ref, h_ref, route_ref, ys_ref, o_ref, b0_ref, b1_ref, sem):
    tm = h_ref.shape[0]

    def issue(r, c):
        _row_copy(ys_ref, dest_ref[0, 0, 2 * r], b0_ref, r, sem).start()
        _row_copy(ys_ref, dest_ref[0, 0, 2 * r + 1], b1_ref, r, sem).start()
        return c

    lax.fori_loop(0, tm, issue, 0)

    def drain(r, c):
        _row_copy(ys_ref, 0, b0_ref, 0, sem).wait()
        _row_copy(ys_ref, 0, b1_ref, 0, sem).wait()
        return c

    lax.fori_loop(0, tm, drain, 0)
    route = route_ref[...]
    o_ref[...] = h_ref[...] + route[:, 2:3] * b0_ref[...] + route[:, 3:4] * b1_ref[...]


def combine(h, route, dest, ys):
    T, D = h.shape
    tm = min(256, T)
    dest3 = dest.reshape(T // tm, 1, 2 * tm)
    return pl.pallas_call(
        _combine_kernel, grid=(T // tm,),
        in_specs=[pl.BlockSpec((1, 1, 2 * tm), lambda i: (i, 0, 0), memory_space=pltpu.SMEM),
                  pl.BlockSpec((tm, D), lambda i: (i, 0)),
                  pl.BlockSpec((tm, LANES), lambda i: (i, 0)),
                  pl.BlockSpec(memory_space=pl.ANY)],
        out_specs=pl.BlockSpec((tm, D), lambda i: (i, 0)),
        out_shape=jax.ShapeDtypeStruct((T, D), F32),
        scratch_shapes=[pltpu.VMEM((tm, D), F32), pltpu.VMEM((tm, D), F32), pltpu.SemaphoreType.DMA(())],
        compiler_params=_cparams(("arbitrary",)), name="moe_combine")(dest3, h, route, ys)


def moe_ffn(h, g, wr, wg, wu, wd):
    T, D = h.shape
    tm = min(512, T)
    route = router(h, g, wr)
    e_flat = route[:, :2].astype(jnp.int32).reshape(2 * T)
    onehot = (e_flat[:, None] == jnp.arange(N_EXPERTS, dtype=jnp.int32)[None, :]).astype(jnp.int32)
    csum = jnp.cumsum(onehot, axis=0)
    counts = csum[-1]
    rank = jnp.sum((csum - 1) * onehot, axis=1)
    tiles = (counts + tm - 1) // tm
    tile_end = jnp.cumsum(tiles)
    offs = (tile_end - tiles) * tm
    dest = (jnp.sum(onehot * offs[None, :], axis=1) + rank).astype(jnp.int32)
    n_tiles = (2 * T) // tm + N_EXPERTS
    n_used = tile_end[-1].astype(jnp.int32)
    tile_ids = jnp.minimum(jnp.arange(n_tiles, dtype=jnp.int32), n_used - 1)
    tile_expert = jnp.sum((tile_ids[:, None] >= tile_end[None, :]).astype(jnp.int32), axis=1).astype(jnp.int32)
    xs = dispatch(h, dest, jnp.zeros((n_tiles * tm, D), F32))
    ys = ffn_grouped(xs, g, wg, wu, wd, tile_expert, n_used.reshape(1), tm)
    return combine(h, route, dest, ys)


def _final_norm_kernel(h_ref, g_ref, o_ref):
    o_ref[...] = _rms(h_ref[...], g_ref[...])


def final_rms(h, g):
    T, D = h.shape
    tm = min(1024, T)
    return pl.pallas_call(
        _final_norm_kernel, grid=(T // tm,),
        in_specs=[pl.BlockSpec((tm, D), lambda i: (i, 0)), pl.BlockSpec((1, D), lambda i: (0, 0))],
        out_specs=pl.BlockSpec((tm, D), lambda i: (i, 0)),
        out_shape=jax.ShapeDtypeStruct((T, D), F32),
        compiler_params=_cparams(("parallel",)), name="final_norm")(h, g)


def kernel(x, positions, attn_norm, w_in, mla_q_norm, mla_w_uq, mla_kv_norm, mla_w_ukv, conv_w, gmlp_ln, gmlp_w_s, gmlp_b_s, nsa_pe_k, nsa_w1_k, nsa_w2_k, nsa_pe_v, nsa_w1_v, nsa_w2_v, group_norm, w_o, ffn_norm, dense_w_gate, dense_w_up, dense_w_down, moe_router, moe_w_gate, moe_w_up, moe_w_down, final_norm):
    B, S, D = x.shape
    T = B * S
    depth = w_in.shape[0]
    h = x.reshape(T, D)
    pos = positions.astype(F32).reshape(T, 1)

    w_in_p = prep_w_in(w_in)
    wqa, wqb, wkv, freq, sign = prep_mla(mla_w_uq, mla_w_ukv)
    conv_w8 = jnp.pad(conv_w, ((0, 0), (0, 8 - CONV_K), (0, 0)))
    b_exp = jnp.repeat(jnp.swapaxes(gmlp_b_s, 1, 2), GROUP_W // GMLP_GROUPS, axis=2)
    sel, place = _head_select(NSA_HEADS)
    gx = _gate_expand()
    msel = cmp_to_sel(S)
    w_o_b = w_o.astype(BF16)
    wr_p = jnp.pad(moe_router, ((0, 0), (0, 0), (0, LANES - N_EXPERTS)))

    for l in range(depth):
        x_mla, x_conv, x_gmlp, x_nq, x_cmp, x_sel, x_win, x_gate = in_proj(h, attn_norm[l][None, :], w_in_p[l])

        q, k, v = mla_proj(x_mla, pos, mla_q_norm[l][None, :], mla_kv_norm[l][None, :],
                           wqa[l], wqb[l], wkv[l], freq, sign)
        y_mla = mla_attn(q, k, v, B, S)

        y_cg = conv_gmlp(x_conv, x_gmlp, conv_w8[l], gmlp_ln[l][None, :], gmlp_w_s[l], b_exp[l], S)

        pea, peb, w1a, w1b, w2 = prep_compress(nsa_pe_k[l], nsa_w1_k[l], nsa_w2_k[l],
                                               nsa_pe_v[l], nsa_w1_v[l], nsa_w2_v[l])
        z2 = x_cmp.reshape(B, S // CMP_STRIDE, CMP_STRIDE * LANES)
        kcv = compress(z2, pea, peb, w1a, w1b, w2)
        o_cmp, sel_bias = nsa_cmp(x_nq, kcv, msel, sel, place, B, S)
        y_nsa = nsa_attn(x_nq, x_sel, x_win, sel_bias, o_cmp, x_gate, sel, place, gx, B, S)

        h = mix_out(y_mla, y_cg, y_nsa, h, group_norm[l].reshape(1, D), w_o_b[l])

        fg = ffn_norm[l][None, :]
        if l % 2 == 0:
            h = ffn_dense(h, fg, dense_w_gate[l // 2].astype(BF16), dense_w_up[l // 2].astype(BF16),
                          dense_w_down[l // 2].astype(BF16))
        else:
            h = moe_ffn(h, fg, wr_p[l // 2], moe_w_gate[l // 2].astype(BF16), moe_w_up[l // 2].astype(BF16),
                        moe_w_down[l // 2].astype(BF16))
    return final_rms(h, final_norm[None, :]).reshape(B, S, D)
```

```python
import functools

import numpy as np
import jax
import jax.numpy as jnp
from jax import lax
from jax.experimental import pallas as pl
from jax.experimental.pallas import tpu as pltpu

F32 = jnp.float32
BF16 = jnp.bfloat16

D_MODEL = 1024
GROUP_W = 256
MLA_HEADS = 4
MLA_NOPE = 64
MLA_ROPE = 32
MLA_V = 64
MLA_Q_RANK = 256
MLA_KV_RANK = 128
ROPE_THETA = 10000.0
CONV_K = 3
GMLP_GROUPS = 4
GMLP_CHUNK = 128
NSA_HEADS = 4
NSA_DH = 64
CMP_LEN = 32
CMP_STRIDE = 16
CMP_HIDDEN = 256
SEL_LEN = 64
N_SELECT = 16
WINDOW = 512
FORCE_SCORE = 1e4
D_FF = 3584
N_EXPERTS = 8
EPS = 1e-6
NEG = -1e30
LOG2E = 1.4426950408889634

LANES = 128
VMEM_LIMIT = 56 * 1024 * 1024

_IN_SIZES = (256, 128, 32, 256, 256, 256, 256, 256, 256, 64, 64, 64, 64, 64, 64, 12)
_IN_OFF = np.concatenate([[0], np.cumsum(_IN_SIZES)]).tolist()

_W_MLA, _W_CONV, _W_GMLP, _W_NQ, _W_KV, _W_GATE = 640, 768, 512, 256, 128, 128
_IN_SPLITS = (_W_MLA, _W_CONV, _W_GMLP, _W_NQ, _W_KV, _W_KV, _W_KV, _W_GATE)
_D_IN_P = sum(_IN_SPLITS)

_NT = (((1,), (1,)), ((), ()))


def _cparams(sem, vmem=VMEM_LIMIT):
    return pltpu.CompilerParams(dimension_semantics=sem, vmem_limit_bytes=vmem)


def _rms(x, g):
    return x * lax.rsqrt(jnp.mean(x * x, axis=-1, keepdims=True) + EPS) * g


def _dot(a, b):
    return jnp.dot(a, b, preferred_element_type=F32)


def _dot_exact(a, b):
    return jnp.dot(a, b, preferred_element_type=F32, precision=lax.Precision.HIGHEST)


def _gelu(x):
    return 0.5 * x * (1.0 + jnp.tanh(0.7978845608028654 * (x + 0.044715 * (x * x * x))))


def _in_proj_kernel(h_ref, g_ref, w_ref, *o_refs):
    xn = _rms(h_ref[...], g_ref[...]).astype(BF16)
    r = _dot(xn, w_ref[...]).astype(BF16)
    off = 0
    for o_ref, w in zip(o_refs, _IN_SPLITS):
        o_ref[...] = r[:, off:off + w]
        off += w


def in_proj(h, g, w):
    T, D = h.shape
    tm = min(512, T)
    return pl.pallas_call(
        _in_proj_kernel, grid=(T // tm,),
        in_specs=[pl.BlockSpec((tm, D), lambda i: (i, 0)),
                  pl.BlockSpec((1, D), lambda i: (0, 0)),
                  pl.BlockSpec((D, _D_IN_P), lambda i: (0, 0))],
        out_specs=[pl.BlockSpec((tm, w_), lambda i: (i, 0)) for w_ in _IN_SPLITS],
        out_shape=[jax.ShapeDtypeStruct((T, w_), BF16) for w_ in _IN_SPLITS],
        compiler_params=_cparams(("parallel",)), name="in_proj")(h, g, w)


def prep_w_in(w_in):
    o = _IN_OFF
    L, D, _ = w_in.shape
    z = lambda n: jnp.zeros((L, D, n), F32)
    kr = w_in[:, :, o[2]:o[3]]
    half = MLA_ROPE // 2
    kr_sw = jnp.concatenate([kr[..., half:], kr[..., :half]], axis=-1)
    cols = [w_in[:, :, o[0]:o[2]],
            kr, z(96), kr_sw, z(96),
            w_in[:, :, o[3]:o[6]],
            w_in[:, :, o[6]:o[8]],
            w_in[:, :, o[8]:o[9]] * (NSA_DH ** -0.5 * LOG2E),
            w_in[:, :, o[9]:o[15]],
            w_in[:, :, o[15]:o[16]], z(_W_GATE - 12)]
    return jnp.concatenate(cols, axis=-1).astype(BF16)


_MLA_QW = 2 * LANES


def _mla_proj_kernel(x_ref, pos_ref, qg_ref, kvg_ref, wqn_ref, wukt_ref, wra_ref, wrb_ref, freq_ref, sign_ref,
                     q_ref, k_ref):
    x = x_ref[...].astype(F32)
    cq, ckv = x[:, :256], x[:, 256:384]
    kra, krb = x[:, 384:512], x[:, 512:640]
    ang = pos_ref[...] * freq_ref[...]
    c = jnp.cos(ang)
    s = jnp.sin(ang) * sign_ref[...]
    cqn = _rms(cq, qg_ref[...]).astype(BF16)
    qn = _dot(cqn, wqn_ref[...])
    ra = _dot(cqn, wra_ref[...])
    rb = _dot(cqn, wrb_ref[...])
    for h in range(MLA_HEADS):
        hs = slice(h * LANES, (h + 1) * LANES)
        q_ref[:, h * _MLA_QW:h * _MLA_QW + LANES] = _dot(qn[:, hs].astype(BF16), wukt_ref[h]).astype(BF16)
        q_ref[:, h * _MLA_QW + LANES:(h + 1) * _MLA_QW] = (ra[:, hs] * c + rb[:, hs] * s).astype(BF16)
    k_ref[:, :LANES] = _rms(ckv, kvg_ref[...]).astype(BF16)
    k_ref[:, LANES:] = (kra * c + krb * s).astype(BF16)


def mla_proj(x_mla, pos, qg, kvg, wqn, wukt, wra, wrb, freq, sign):
    T = x_mla.shape[0]
    tm = min(512, T)
    full = lambda a: pl.BlockSpec(a.shape, lambda i: (0,) * a.ndim)
    return pl.pallas_call(
        _mla_proj_kernel, grid=(T // tm,),
        in_specs=[pl.BlockSpec((tm, _W_MLA), lambda i: (i, 0)),
                  pl.BlockSpec((tm, 1), lambda i: (i, 0)),
                  full(qg), full(kvg), full(wqn), full(wukt), full(wra), full(wrb), full(freq), full(sign)],
        out_specs=[pl.BlockSpec((tm, MLA_HEADS * _MLA_QW), lambda i: (i, 0)),
                   pl.BlockSpec((tm, _MLA_QW), lambda i: (i, 0))],
        out_shape=[jax.ShapeDtypeStruct((T, MLA_HEADS * _MLA_QW), BF16),
                   jax.ShapeDtypeStruct((T, _MLA_QW), BF16)],
        compiler_params=_cparams(("parallel",)), name="mla_proj")(
            x_mla, pos, qg, kvg, wqn, wukt, wra, wrb, freq, sign)


def prep_mla(w_uq, w_ukv):
    L = w_uq.shape[0]
    scale = (MLA_NOPE + MLA_ROPE) ** -0.5 * LOG2E
    half = MLA_ROPE // 2
    hq = MLA_NOPE + MLA_ROPE
    hk = MLA_NOPE + MLA_V
    zq = lambda n: jnp.zeros((L, MLA_Q_RANK, n), F32)
    qn, ra, rb, ukt, uvp = [], [], [], [], []
    for h in range(MLA_HEADS):
        rp = w_uq[:, :, h * hq + MLA_NOPE:(h + 1) * hq]
        qn += [w_uq[:, :, h * hq:h * hq + MLA_NOPE], zq(LANES - MLA_NOPE)]
        ra += [rp, zq(LANES - MLA_ROPE)]
        rb += [jnp.concatenate([rp[..., half:], rp[..., :half]], axis=-1), zq(LANES - MLA_ROPE)]
        w_uk = w_ukv[:, :, h * hk:h * hk + MLA_NOPE]
        ukt.append(jnp.pad(jnp.swapaxes(w_uk, 1, 2), ((0, 0), (0, LANES - MLA_NOPE), (0, 0))))
        w_uv = w_ukv[:, :, h * hk + MLA_NOPE:(h + 1) * hk]
        uvp.append(jnp.pad(w_uv, ((0, 0), (0, 0), (h * MLA_V, (MLA_HEADS - 1 - h) * MLA_V))))
    wqn = (jnp.concatenate(qn, axis=-1) * scale).astype(BF16)
    wra = (jnp.concatenate(ra, axis=-1) * scale).astype(BF16)
    wrb = (jnp.concatenate(rb, axis=-1) * scale).astype(BF16)
    wukt = jnp.stack(ukt, axis=1).astype(BF16)
    wuvp = jnp.stack(uvp, axis=1).astype(BF16)
    inv = ROPE_THETA ** (-jnp.arange(half, dtype=F32) / half)
    pad = jnp.zeros((LANES - MLA_ROPE,), F32)
    freq = jnp.concatenate([inv, inv, pad])[None, :]
    sign = jnp.concatenate([-jnp.ones((half,), F32), jnp.ones((half,), F32), pad])[None, :]
    return wqn, wukt, wra, wrb, wuvp, freq, sign


def _causal_mask(t):
    row = lax.broadcasted_iota(jnp.int32, (t, t), 0)
    col = lax.broadcasted_iota(jnp.int32, (t, t), 1)
    return row >= col


def _lane_fold(x, op):
    r = x[:, :LANES]
    for c in range(1, x.shape[1] // LANES):
        r = op(r, x[:, c * LANES:(c + 1) * LANES])
    return r


class _Softmax:
    def __init__(self, q, m_ref, l_ref, acc_ref):
        self.q, self.m_ref, self.l_ref, self.acc_ref = q, m_ref, l_ref, acc_ref
        m_ref[...] = jnp.full(m_ref.shape, NEG, F32)
        l_ref[...] = jnp.zeros(l_ref.shape, F32)
        acc_ref[...] = jnp.zeros(acc_ref.shape, F32)

    def tile(self, k, v, mask=None):
        s = lax.dot_general(self.q, k, _NT, preferred_element_type=F32)
        if mask is not None:
            s = jnp.where(mask, s, NEG)
        m_old = self.m_ref[...]
        m_new = jnp.maximum(m_old, jnp.max(_lane_fold(s, jnp.maximum), axis=-1, keepdims=True))
        alpha = jnp.exp2(m_old - m_new)
        parts = [jnp.exp2(s[:, c * LANES:(c + 1) * LANES] - m_new) for c in range(s.shape[1] // LANES)]
        lsum = parts[0]
        for p in parts[1:]:
            lsum = lsum + p
        self.l_ref[...] = alpha * self.l_ref[...] + lsum
        p = jnp.concatenate(parts, axis=1).astype(BF16)
        self.acc_ref[...] = alpha * self.acc_ref[...] + _dot(p, v)
        self.m_ref[...] = m_new

    def result(self):
        return self.acc_ref[...] / jnp.sum(self.l_ref[...], axis=-1, keepdims=True)


def _stacked_causal(nh, tq, tk, q0, k0):
    row = lax.broadcasted_iota(jnp.int32, (nh * tq, tk), 0) & (tq - 1)
    col = lax.broadcasted_iota(jnp.int32, (nh * tq, tk), 1)
    return (k0 + col) - (q0 + row)


def _mla_attn_kernel(q_ref, k_ref, wuvp_ref, o_ref, qs_ref, m_ref, l_ref, acc_ref, *, tq, tk):
    i = pl.program_id(1)
    for h in range(MLA_HEADS):
        qs_ref[h * tq:(h + 1) * tq, :] = q_ref[:, h * _MLA_QW:(h + 1) * _MLA_QW]
    sm = _Softmax(qs_ref[...], m_ref, l_ref, acc_ref)
    n_full = (i * tq) // tk

    def body(j, c):
        kj = k_ref[pl.ds(pl.multiple_of(j * tk, tk), tk), :]
        sm.tile(kj, kj[:, :LANES])
        return c

    lax.fori_loop(0, n_full, body, 0)
    kd = k_ref[pl.ds(pl.multiple_of(n_full * tk, tk), tk), :]
    sm.tile(kd, kd[:, :LANES], _stacked_causal(MLA_HEADS, tq, tk, i * tq, n_full * tk) <= 0)
    o_lat = sm.result().astype(BF16)
    out = _dot(o_lat[:tq], wuvp_ref[0])
    for h in range(1, MLA_HEADS):
        out = out + _dot(o_lat[h * tq:(h + 1) * tq], wuvp_ref[h])
    o_ref[...] = out.astype(BF16)


_ATT_TQ, _ATT_TK = 256, 512


def mla_attn(q, k, wuvp, B, S):
    tq, tk = min(_ATT_TQ, S), min(_ATT_TK, S)
    assert tq & (tq - 1) == 0
    M = MLA_HEADS * tq
    q3, k3 = q.reshape(B, S, MLA_HEADS * _MLA_QW), k.reshape(B, S, _MLA_QW)
    out = pl.pallas_call(
        functools.partial(_mla_attn_kernel, tq=tq, tk=tk), grid=(B, S // tq),
        in_specs=[pl.BlockSpec((None, tq, MLA_HEADS * _MLA_QW), lambda b, i: (b, i, 0)),
                  pl.BlockSpec((None, S, _MLA_QW), lambda b, i: (b, 0, 0)),
                  pl.BlockSpec(wuvp.shape, lambda b, i: (0, 0, 0))],
        out_specs=pl.BlockSpec((None, tq, 256), lambda b, i: (b, i, 0)),
        out_shape=jax.ShapeDtypeStruct((B, S, 256), BF16),
        scratch_shapes=[pltpu.VMEM((M, _MLA_QW), BF16), pltpu.VMEM((M, LANES), F32),
                        pltpu.VMEM((M, LANES), F32), pltpu.VMEM((M, LANES), F32)],
        compiler_params=_cparams(("parallel", "arbitrary")), name="mla_attn")(q3, k3, wuvp)
    return out.reshape(B * S, 256)


def _conv_gmlp_kernel(xc_ref, halo_ref, xg_ref, cw_ref, ln_ref, ws_ref, bs_ref, o_ref, *, tiles_per_seq):
    i = pl.program_id(0)
    tm = xc_ref.shape[0]
    xc = xc_ref[...].astype(F32)
    cb, u = xc[:, :256], xc[:, 256:512] * xc[:, 512:768]
    hx = halo_ref[...].astype(F32)
    keep = jnp.where(i % tiles_per_seq == 0, 0.0, 1.0)
    hu = hx[:, 256:512] * hx[:, 512:768] * keep
    p1, p2 = hu[7:8, :], hu[6:7, :]
    row = lax.broadcasted_iota(jnp.int32, (tm, 256), 0)
    u1 = jnp.where(row == 0, p1, pltpu.roll(u, 1, axis=0))
    u2 = jnp.where(row == 0, p2, jnp.where(row == 1, p1, pltpu.roll(u, 2, axis=0)))
    cw = cw_ref[...]
    y_conv = cb * (cw[0:1, :] * u2 + cw[1:2, :] * u1 + cw[2:3, :] * u)
    o_ref[:, :256] = y_conv.astype(BF16)

    xg = xg_ref[...].astype(F32)
    gu = _gelu(xg[:, :256])
    gv = _gelu(xg[:, 256:512])
    mu = jnp.mean(gv, axis=-1, keepdims=True)
    var = jnp.mean(jnp.square(gv - mu), axis=-1, keepdims=True)
    vn = (gv - mu) * lax.rsqrt(var + EPS) * ln_ref[...]
    lane_g = lax.broadcasted_iota(jnp.int32, (GMLP_CHUNK, 256), 1) // (256 // GMLP_GROUPS)
    tril = _causal_mask(GMLP_CHUNK)
    for c in range(tm // GMLP_CHUNK):
        vc = vn[c * GMLP_CHUNK:(c + 1) * GMLP_CHUNK, :]
        sv = bs_ref[...]
        for g in range(GMLP_GROUPS):
            wg = jnp.where(tril, ws_ref[g], 0.0).astype(BF16)
            sv = sv + _dot(wg, jnp.where(lane_g == g, vc, 0.0).astype(BF16))
        o_ref[c * GMLP_CHUNK:(c + 1) * GMLP_CHUNK, 256:512] = (
            gu[c * GMLP_CHUNK:(c + 1) * GMLP_CHUNK, :] * sv).astype(BF16)


def conv_gmlp(x_conv, x_gmlp, conv_w8, ln_g, w_s, b_exp, S):
    T = x_conv.shape[0]
    tm = min(512, S)
    full = lambda a: pl.BlockSpec(a.shape, lambda i: (0,) * a.ndim)
    return pl.pallas_call(
        functools.partial(_conv_gmlp_kernel, tiles_per_seq=S // tm), grid=(T // tm,),
        in_specs=[pl.BlockSpec((tm, _W_CONV), lambda i: (i, 0)),
                  pl.BlockSpec((8, _W_CONV), lambda i: (jnp.maximum(i * (tm // 8) - 1, 0), 0)),
                  pl.BlockSpec((tm, _W_GMLP), lambda i: (i, 0)),
                  full(conv_w8), full(ln_g), full(w_s), full(b_exp)],
        out_specs=pl.BlockSpec((tm, 512), lambda i: (i, 0)),
        out_shape=jax.ShapeDtypeStruct((T, 512), BF16),
        compiler_params=_cparams(("parallel",)), name="conv_gmlp")(
            x_conv, x_conv, x_gmlp, conv_w8, ln_g, w_s, b_exp)


def _compress_kernel(z_ref, pea_ref, peb_ref, w1a_ref, w1b_ref, w2_ref, o_ref):
    z = z_ref[...]
    n = z.shape[0]
    a = _dot(z, w1a_ref[...])
    b = _dot(z, w1b_ref[...])
    c = _dot_exact(pea_ref[...], w1a_ref[...].astype(F32)) + _dot_exact(peb_ref[...], w1b_ref[...].astype(F32))
    pre = a + pltpu.roll(b, n - 1, axis=0) + c[0:1, :]
    o_ref[...] = _dot(_gelu(pre).astype(BF16), w2_ref[...]).astype(BF16)


def compress(z2, pea, peb, w1a, w1b, w2):
    B, n, K = z2.shape
    full = lambda a: pl.BlockSpec(a.shape, lambda b: (0,) * a.ndim)
    return pl.pallas_call(
        _compress_kernel, grid=(B,),
        in_specs=[pl.BlockSpec((None, n, K), lambda b: (b, 0, 0)),
                  full(pea), full(peb), full(w1a), full(w1b), full(w2)],
        out_specs=pl.BlockSpec((None, n, LANES), lambda b: (b, 0, 0)),
        out_shape=jax.ShapeDtypeStruct((B, n, LANES), BF16),
        compiler_params=_cparams(("parallel",)), name="nsa_compress")(z2, pea, peb, w1a, w1b, w2)


def prep_compress(pe_k, w1_k, w2_k, pe_v, w1_v, w2_v):
    half = CMP_LEN // 2

    def w1_half(w1k, w1v):
        wk = w1k.reshape(half, NSA_DH, CMP_HIDDEN)
        wv = w1v.reshape(half, NSA_DH, CMP_HIDDEN)
        zk = jnp.zeros_like(wk)
        top = jnp.concatenate([wk, zk], axis=-1)
        bot = jnp.concatenate([zk, wv], axis=-1)
        return jnp.concatenate([top, bot], axis=1).reshape(half * 2 * NSA_DH, 2 * CMP_HIDDEN)

    n1 = half * NSA_DH
    w1a = w1_half(w1_k[:n1], w1_v[:n1]).astype(BF16)
    w1b = w1_half(w1_k[n1:], w1_v[n1:]).astype(BF16)

    def pe_half(pk, pv):
        row = jnp.concatenate([pk, pv], axis=-1).reshape(1, half * 2 * NSA_DH)
        return jnp.broadcast_to(row, (8, half * 2 * NSA_DH))

    pea = pe_half(pe_k[:half], pe_v[:half])
    peb = pe_half(pe_k[half:], pe_v[half:])
    z = jnp.zeros((CMP_HIDDEN, NSA_DH), F32)
    w2 = jnp.concatenate([jnp.concatenate([w2_k, z], axis=1),
                          jnp.concatenate([z, w2_v], axis=1)], axis=0).astype(BF16)
    return pea, peb, w1a, w1b, w2


def _head_select(nh):
    sel = np.zeros((nh, nh * NSA_DH, LANES), np.float32)
    place = np.zeros((nh, LANES, nh * NSA_DH), np.float32)
    for h in range(nh):
        for d in range(NSA_DH):
            sel[h, h * NSA_DH + d, d] = 1.0
            place[h, NSA_DH + d, h * NSA_DH + d] = 1.0
    return jnp.asarray(sel, BF16), jnp.asarray(place, BF16)


def _stack_heads(q, sel_ref, qs_ref):
    for h in range(NSA_HEADS):
        qs_ref[h] = _dot(q, sel_ref[h]).astype(BF16)


def _place_heads(o_list, place_ref):
    out = _dot(o_list[0].astype(BF16), place_ref[0])
    for h in range(1, NSA_HEADS):
        out = out + _dot(o_list[h].astype(BF16), place_ref[h])
    return out


def _nsa_cmp_kernel(q_ref, kv_ref, msel_ref, sel_ref, place_ref, o_ref, bias_ref, qs_ref, *, t, ns):
    i = pl.program_id(1)
    nc = kv_ref.shape[0]
    _stack_heads(q_ref[...], sel_ref, qs_ref)
    kv = kv_ref[...]
    tpos = i * t + lax.broadcasted_iota(jnp.int32, (t, nc), 0)
    blk_end = lax.broadcasted_iota(jnp.int32, (t, nc), 1) * CMP_STRIDE + (CMP_LEN - 1)
    cmask = blk_end <= tpos
    p_sum = jnp.zeros((t, nc), F32)
    outs = []
    for h in range(NSA_HEADS):
        s = lax.dot_general(qs_ref[h], kv, _NT, preferred_element_type=F32)
        s = jnp.where(cmask, s, NEG)
        m = jnp.max(s, axis=-1, keepdims=True)
        e = jnp.where(cmask, jnp.exp2(s - m), 0.0)
        l = jnp.sum(e, axis=-1, keepdims=True)
        p = e / jnp.where(l > 0.0, l, 1.0)
        p_sum = p_sum + p
        outs.append(_dot(p.astype(BF16), kv))
    o_ref[...] = _place_heads(outs, place_ref).astype(BF16)

    p_sel = _dot_exact(p_sum, msel_ref[...])
    blk = lax.broadcasted_iota(jnp.int32, (t, LANES), 1) - NSA_DH
    blk_f = blk.astype(F32)
    cur = (i * t + lax.broadcasted_iota(jnp.int32, (t, LANES), 0)) // SEL_LEN
    forced = (blk == 0) | (blk == cur) | (blk == cur - 1)
    v = jnp.where(forced, FORCE_SCORE, jnp.where(blk <= cur, p_sel, -1.0))
    v = jnp.where((blk >= 0) & (blk < ns), v, -jnp.inf)
    chosen = blk < 0
    for _ in range(min(N_SELECT, ns)):
        mx = jnp.max(v, axis=-1, keepdims=True)
        idx = jnp.min(jnp.where(v == mx, blk_f, float(LANES)), axis=-1, keepdims=True)
        hit = blk_f == idx
        chosen = chosen | hit
        v = jnp.where(hit, -jnp.inf, v)
    bias_ref[...] = jnp.where(chosen, 0.0, NEG).astype(BF16)


def nsa_cmp(q, kcv, msel, sel, place, B, S):
    t = min(1024, S)
    nc = kcv.shape[1]
    ns = S // SEL_LEN
    full = lambda a: pl.BlockSpec(a.shape, lambda b, i: (0,) * a.ndim)
    return pl.pallas_call(
        functools.partial(_nsa_cmp_kernel, t=t, ns=ns), grid=(B, S // t),
        in_specs=[pl.BlockSpec((None, t, 256), lambda b, i: (b, i, 0)),
                  pl.BlockSpec((None, nc, LANES), lambda b, i: (b, 0, 0)),
                  full(msel), full(sel), full(place)],
        out_specs=[pl.BlockSpec((None, t, 256), lambda b, i: (b, i, 0)),
                   pl.BlockSpec((None, t, LANES), lambda b, i: (b, i, 0))],
        out_shape=[jax.ShapeDtypeStruct((B, S, 256), BF16), jax.ShapeDtypeStruct((B, S, LANES), BF16)],
        scratch_shapes=[pltpu.VMEM((NSA_HEADS, t, LANES), BF16)],
        compiler_params=_cparams(("parallel", "arbitrary")), name="nsa_cmp")(
            q.reshape(B, S, 256), kcv, msel, sel, place)


def cmp_to_sel(S):
    nc = S // CMP_STRIDE - CMP_LEN // CMP_STRIDE + 1
    ns = S // SEL_LEN
    cs = np.arange(nc)[:, None] * CMP_STRIDE
    ss = np.arange(ns)[None, :] * SEL_LEN
    ov = np.clip(np.minimum(cs + CMP_LEN, ss + SEL_LEN) - np.maximum(cs, ss), 0, None)
    assert ns <= LANES - NSA_DH, "selection blocks must fit the lanes beside one head's query"
    m = np.zeros((S // CMP_STRIDE, LANES), np.float32)
    m[:nc, NSA_DH:NSA_DH + ns] = ov.astype(np.float32) / np.float32(CMP_LEN)
    return jnp.asarray(m)


def _nsa_attn_kernel(q_ref, ksel_ref, kwin_ref, bias_ref, ocmp_ref, gl_ref, sel_ref, place_ref, gx_ref,
                     o_ref, kx_ref, qs_ref, qw_ref, m_ref, l_ref, acc_ref, *, tq, tk, wlen):
    i = pl.program_id(1)
    S = ksel_ref.shape[0]

    @pl.when(i == 0)
    def _():
        r = lax.broadcasted_iota(jnp.int32, (S, LANES), 0)
        c = lax.broadcasted_iota(jnp.int32, (S, LANES), 1)
        onehot = jnp.where(r // SEL_LEN == c - NSA_DH, 1.0, 0.0)
        kx_ref[...] = jnp.where(c < NSA_DH, ksel_ref[...].astype(F32), onehot).astype(BF16)

    q = q_ref[...]
    bias = bias_ref[...].astype(F32)
    for h in range(NSA_HEADS):
        qs = _dot(q, sel_ref[h])
        qw_ref[h * tq:(h + 1) * tq, :] = qs.astype(BF16)
        qs_ref[h * tq:(h + 1) * tq, :] = (qs + bias).astype(BF16)

    def heads(x):
        return [x[h * tq:(h + 1) * tq] for h in range(NSA_HEADS)]

    sm = _Softmax(qs_ref[...], m_ref, l_ref, acc_ref)
    n_full = (i * tq) // tk

    def body(j, c):
        kj = pl.ds(pl.multiple_of(j * tk, tk), tk)
        sm.tile(kx_ref[kj, :], ksel_ref[kj, :])
        return c

    lax.fori_loop(0, n_full, body, 0)
    kd = pl.ds(pl.multiple_of(n_full * tk, tk), tk)
    sm.tile(kx_ref[kd, :], ksel_ref[kd, :], _stacked_causal(NSA_HEADS, tq, tk, i * tq, n_full * tk) <= 0)
    o_sel = _place_heads(heads(sm.result()), place_ref)

    sm = _Softmax(qw_ref[...], m_ref, l_ref, acc_ref)
    w0 = pl.multiple_of(jnp.maximum(i * tq + tq - wlen, 0), tq)
    diff = _stacked_causal(NSA_HEADS, tq, wlen, i * tq, w0)
    kw = kwin_ref[pl.ds(w0, wlen), :]
    sm.tile(kw, kw, (diff <= 0) & (diff > -WINDOW))
    o_win = _place_heads(heads(sm.result()), place_ref)

    g = 1.0 / (1.0 + jnp.exp(-gl_ref[...].astype(F32)))
    o = (_dot_exact(g, gx_ref[0]) * ocmp_ref[...].astype(F32)
         + _dot_exact(g, gx_ref[1]) * o_sel + _dot_exact(g, gx_ref[2]) * o_win)
    o_ref[...] = o.astype(BF16)


def nsa_attn(q, ksel, kwin, bias, o_cmp, gl, sel, place, gx, B, S):
    tq, tk = min(_ATT_TQ, S), min(_ATT_TK, S)
    wlen = min(WINDOW + tq, S)
    full = lambda a: pl.BlockSpec(a.shape, lambda b, i: (0,) * a.ndim)
    tile2 = lambda w: pl.BlockSpec((None, tq, w), lambda b, i: (b, i, 0))
    seq = pl.BlockSpec((None, S, LANES), lambda b, i: (b, 0, 0))
    out = pl.pallas_call(
        functools.partial(_nsa_attn_kernel, tq=tq, tk=tk, wlen=wlen), grid=(B, S // tq),
        in_specs=[tile2(256), seq, seq, tile2(LANES), tile2(256), tile2(LANES), full(sel), full(place), full(gx)],
        out_specs=tile2(256),
        out_shape=jax.ShapeDtypeStruct((B, S, 256), BF16),
        scratch_shapes=[pltpu.VMEM((S, LANES), BF16),
                        pltpu.VMEM((NSA_HEADS * tq, LANES), BF16), pltpu.VMEM((NSA_HEADS * tq, LANES), BF16),
                        pltpu.VMEM((NSA_HEADS * tq, LANES), F32), pltpu.VMEM((NSA_HEADS * tq, LANES), F32),
                        pltpu.VMEM((NSA_HEADS * tq, LANES), F32)],
        compiler_params=_cparams(("parallel", "arbitrary")), name="nsa_attn")(
            q.reshape(B, S, 256), ksel.reshape(B, S, LANES), kwin.reshape(B, S, LANES), bias, o_cmp,
            gl.reshape(B, S, LANES), sel, place, gx)
    return out.reshape(B * S, 256)


def _gate_expand():
    gx = np.zeros((3, LANES, NSA_HEADS * NSA_DH), np.float32)
    for br in range(3):
        for h in range(NSA_HEADS):
            gx[br, h * 3 + br, h * NSA_DH:(h + 1) * NSA_DH] = 1.0
    return jnp.asarray(gx)


def _mix_out_kernel(ymla_ref, ycg_ref, ynsa_ref, h_ref, gn_ref, wo_ref, o_ref):
    gn = gn_ref[...]
    parts = [ymla_ref[...].astype(F32), ycg_ref[:, :256].astype(F32), ycg_ref[:, 256:].astype(F32),
             ynsa_ref[...].astype(F32)]
    yn = [_rms(p, gn[:, k * GROUP_W:(k + 1) * GROUP_W]).astype(BF16) for k, p in enumerate(parts)]
    o_ref[...] = h_ref[...] + _dot(jnp.concatenate(yn, axis=1), wo_ref[...])


def mix_out(y_mla, y_cg, y_nsa, h, gn, wo):
    T, D = h.shape
    tm = min(512, T)
    row = lambda w: pl.BlockSpec((tm, w), lambda i: (i, 0))
    return pl.pallas_call(
        _mix_out_kernel, grid=(T // tm,),
        in_specs=[row(256), row(512), row(256), row(D),
                  pl.BlockSpec((1, D), lambda i: (0, 0)), pl.BlockSpec((D, D), lambda i: (0, 0))],
        out_specs=row(D), out_shape=jax.ShapeDtypeStruct((T, D), F32),
        compiler_params=_cparams(("parallel",)), name="mix_out")(y_mla, y_cg, y_nsa, h, gn, wo)


def _ffn_step(j, nj, x_ref, g_ref, wg_ref, wu_ref, wd_ref, o_ref, hn_ref, acc_ref, residual):
    @pl.when(j == 0)
    def _():
        hn_ref[...] = _rms(x_ref[...], g_ref[...]).astype(BF16)
        acc_ref[...] = jnp.zeros(acc_ref.shape, F32)

    hn = hn_ref[...]
    a = _dot(hn, wg_ref[...])
    b = _dot(hn, wu_ref[...])
    act = (a / (1.0 + jnp.exp(-a)) * b).astype(BF16)
    acc_ref[...] += _dot(act, wd_ref[...])

    @pl.when(j == nj - 1)
    def _():
        o_ref[...] = (x_ref[...] + acc_ref[...]) if residual else acc_ref[...]


def _ffn_kernel(x_ref, g_ref, wg_ref, wu_ref, wd_ref, o_ref, hn_ref, acc_ref):
    _ffn_step(pl.program_id(1), pl.num_programs(1), x_ref, g_ref, wg_ref, wu_ref, wd_ref, o_ref, hn_ref,
              acc_ref, True)


_FFN_TF = 512


def ffn_dense(h, g, wg, wu, wd):
    T, D = h.shape
    F = wg.shape[1]
    tm = min(1024, T)
    return pl.pallas_call(
        _ffn_kernel, grid=(T // tm, F // _FFN_TF),
        in_specs=[pl.BlockSpec((tm, D), lambda i, j: (i, 0)),
                  pl.BlockSpec((1, D), lambda i, j: (0, 0)),
                  pl.BlockSpec((D, _FFN_TF), lambda i, j: (0, j)),
                  pl.BlockSpec((D, _FFN_TF), lambda i, j: (0, j)),
                  pl.BlockSpec((_FFN_TF, D), lambda i, j: (j, 0))],
        out_specs=pl.BlockSpec((tm, D), lambda i, j: (i, 0)),
        out_shape=jax.ShapeDtypeStruct((T, D), F32),
        scratch_shapes=[pltpu.VMEM((tm, D), BF16), pltpu.VMEM((tm, D), F32)],
        compiler_params=_cparams(("parallel", "arbitrary")), name="ffn_dense")(h, g, wg, wu, wd)


def _ffn_grouped_kernel(te_ref, nu_ref, x_ref, g_ref, wg_ref, wu_ref, wd_ref, o_ref, hn_ref, acc_ref):
    del te_ref
    i, j, nj = pl.program_id(0), pl.program_id(1), pl.num_programs(1)
    used = i < nu_ref[0]

    @pl.when(used)
    def _():
        _ffn_step(j, nj, x_ref, g_ref, wg_ref, wu_ref, wd_ref, o_ref, hn_ref, acc_ref, False)

    @pl.when(jnp.logical_not(used))
    def _():
        o_ref[...] = jnp.zeros(o_ref.shape, F32)


def ffn_grouped(xs, g, wg, wu, wd, tile_expert, n_used, tm):
    NP, D = xs.shape
    F = wg.shape[2]
    nf = F // _FFN_TF

    def jj(i, j, nu):
        return jnp.where(i < nu[0], j, nf - 1)

    grid_spec = pltpu.PrefetchScalarGridSpec(
        num_scalar_prefetch=2, grid=(NP // tm, nf),
        in_specs=[pl.BlockSpec((tm, D), lambda i, j, te, nu: (i, 0)),
                  pl.BlockSpec((1, D), lambda i, j, te, nu: (0, 0)),
                  pl.BlockSpec((None, D, _FFN_TF), lambda i, j, te, nu: (te[i], 0, jj(i, j, nu))),
                  pl.BlockSpec((None, D, _FFN_TF), lambda i, j, te, nu: (te[i], 0, jj(i, j, nu))),
                  pl.BlockSpec((None, _FFN_TF, D), lambda i, j, te, nu: (te[i], jj(i, j, nu), 0))],
        out_specs=pl.BlockSpec((tm, D), lambda i, j, te, nu: (i, 0)),
        scratch_shapes=[pltpu.VMEM((tm, D), BF16), pltpu.VMEM((tm, D), F32)])
    return pl.pallas_call(
        _ffn_grouped_kernel, grid_spec=grid_spec,
        out_shape=jax.ShapeDtypeStruct((NP, D), F32),
        compiler_params=_cparams(("arbitrary", "arbitrary")), name="ffn_grouped")(
            tile_expert, n_used, xs, g, wg, wu, wd)


def _router_kernel(h_ref, g_ref, wr_ref, o_ref):
    hn = _rms(h_ref[...], g_ref[...])
    logits = _dot_exact(hn, wr_ref[...])
    tm = logits.shape[0]
    lane = lax.broadcasted_iota(jnp.int32, (tm, LANES), 1)
    lane_f = lane.astype(F32)
    lg = jnp.where(lane < N_EXPERTS, logits, -jnp.inf)
    m1 = jnp.max(lg, axis=-1, keepdims=True)
    i1 = jnp.min(jnp.where(lg == m1, lane_f, float(LANES)), axis=-1, keepdims=True)
    lg2 = jnp.where(lane_f == i1, -jnp.inf, lg)
    m2 = jnp.max(lg2, axis=-1, keepdims=True)
    i2 = jnp.min(jnp.where(lg2 == m2, lane_f, float(LANES)), axis=-1, keepdims=True)
    e = jnp.exp(m2 - m1)
    w1 = 1.0 / (1.0 + e)
    w2 = e / (1.0 + e)
    o_ref[...] = jnp.where(lane == 0, i1,
                           jnp.where(lane == 1, i2,
                                     jnp.where(lane == 2, w1, jnp.where(lane == 3, w2, 0.0))))


def router(h, g, wr):
    T, D = h.shape
    tm = min(512, T)
    return pl.pallas_call(
        _router_kernel, grid=(T // tm,),
        in_specs=[pl.BlockSpec((tm, D), lambda i: (i, 0)), pl.BlockSpec((1, D), lambda i: (0, 0)),
                  pl.BlockSpec((D, LANES), lambda i: (0, 0))],
        out_specs=pl.BlockSpec((tm, LANES), lambda i: (i, 0)),
        out_shape=jax.ShapeDtypeStruct((T, LANES), F32),
        compiler_params=_cparams(("parallel",)), name="moe_router")(h, g, wr)


def _row_copy(src, s, dst, d, sem):
    return pltpu.make_async_copy(src.at[pl.ds(s, 1)], dst.at[pl.ds(d, 1)], sem)


def _dispatch_kernel(dest_ref, h_ref, xs_in_ref, xs_ref, sem):
    del xs_in_ref
    tm = h_ref.shape[0]

    def issue(r, c):
        _row_copy(h_ref, r, xs_ref, dest_ref[0, 0, 2 * r], sem).start()
        _row_copy(h_ref, r, xs_ref, dest_ref[0, 0, 2 * r + 1], sem).start()
        return c

    lax.fori_loop(0, tm, issue, 0, unroll=8)

    def drain(r, c):
        _row_copy(h_ref, 0, xs_ref, 0, sem).wait()
        _row_copy(h_ref, 0, xs_ref, 0, sem).wait()
        return c

    lax.fori_loop(0, tm, drain, 0, unroll=8)


def dispatch(h, dest, xs_zero):
    T, D = h.shape
    tm = min(512, T)
    dest3 = dest.reshape(T // tm, 1, 2 * tm)
    return pl.pallas_call(
        _dispatch_kernel, grid=(T // tm,),
        in_specs=[pl.BlockSpec((1, 1, 2 * tm), lambda i: (i, 0, 0), memory_space=pltpu.SMEM),
                  pl.BlockSpec((tm, D), lambda i: (i, 0)),
                  pl.BlockSpec(memory_space=pl.ANY)],
        out_specs=pl.BlockSpec(memory_space=pl.ANY),
        out_shape=jax.ShapeDtypeStruct(xs_zero.shape, F32),
        scratch_shapes=[pltpu.SemaphoreType.DMA(())],
        input_output_aliases={2: 0},
        compiler_params=_cparams(("arbitrary",)), name="moe_dispatch")(dest3, h, xs_zero)


def _combine_kernel(dest_ref, h_ref, route_ref, ys_ref, o_ref, b0_ref, b1_ref, sem):
    tm = h_ref.shape[0]

    def issue(r, c):
        _row_copy(ys_ref, dest_ref[0, 0, 2 * r], b0_ref, r, sem).start()
        _row_copy(ys_ref, dest_ref[0, 0, 2 * r + 1], b1_ref, r, sem).start()
        return c

    lax.fori_loop(0, tm, issue, 0, unroll=8)

    def drain(r, c):
        _row_copy(ys_ref, 0, b0_ref, 0, sem).wait()
        _row_copy(ys_ref, 0, b1_ref, 0, sem).wait()
        return c

    lax.fori_loop(0, tm, drain, 0, unroll=8)
    route = route_ref[...]
    o_ref[...] = h_ref[...] + route[:, 2:3] * b0_ref[...] + route[:, 3:4] * b1_ref[...]


def combine(h, route, dest, ys):
    T, D = h.shape
    tm = min(256, T)
    dest3 = dest.reshape(T // tm, 1, 2 * tm)
    return pl.pallas_call(
        _combine_kernel, grid=(T // tm,),
        in_specs=[pl.BlockSpec((1, 1, 2 * tm), lambda i: (i, 0, 0), memory_space=pltpu.SMEM),
                  pl.BlockSpec((tm, D), lambda i: (i, 0)),
                  pl.BlockSpec((tm, LANES), lambda i: (i, 0)),
                  pl.BlockSpec(memory_space=pl.ANY)],
        out_specs=pl.BlockSpec((tm, D), lambda i: (i, 0)),
        out_shape=jax.ShapeDtypeStruct((T, D), F32),
        scratch_shapes=[pltpu.VMEM((tm, D), F32), pltpu.VMEM((tm, D), F32), pltpu.SemaphoreType.DMA(())],
        compiler_params=_cparams(("arbitrary",)), name="moe_combine")(dest3, h, route, ys)


def moe_ffn(h, g, wr, wg, wu, wd):
    T, D = h.shape
    tm = min(512, T)
    route = router(h, g, wr)
    e_flat = route[:, :2].astype(jnp.int32).reshape(2 * T)
    onehot = (e_flat[:, None] == jnp.arange(N_EXPERTS, dtype=jnp.int32)[None, :]).astype(jnp.int32)
    csum = jnp.cumsum(onehot, axis=0)
    counts = csum[-1]
    rank = jnp.sum((csum - 1) * onehot, axis=1)
    tiles = (counts + tm - 1) // tm
    tile_end = jnp.cumsum(tiles)
    offs = (tile_end - tiles) * tm
    dest = (jnp.sum(onehot * offs[None, :], axis=1) + rank).astype(jnp.int32)
    n_tiles = (2 * T) // tm + N_EXPERTS
    n_used = tile_end[-1].astype(jnp.int32)
    tile_ids = jnp.minimum(jnp.arange(n_tiles, dtype=jnp.int32), n_used - 1)
    tile_expert = jnp.sum((tile_ids[:, None] >= tile_end[None, :]).astype(jnp.int32), axis=1).astype(jnp.int32)
    xs = dispatch(h, dest, jnp.zeros((n_tiles * tm, D), F32))
    ys = ffn_grouped(xs, g, wg, wu, wd, tile_expert, n_used.reshape(1), tm)
    return combine(h, route, dest, ys)


def _final_norm_kernel(h_ref, g_ref, o_ref):
    o_ref[...] = _rms(h_ref[...], g_ref[...])


def final_rms(h, g):
    T, D = h.shape
    tm = min(1024, T)
    return pl.pallas_call(
        _final_norm_kernel, grid=(T // tm,),
        in_specs=[pl.BlockSpec((tm, D), lambda i: (i, 0)), pl.BlockSpec((1, D), lambda i: (0, 0))],
        out_specs=pl.BlockSpec((tm, D), lambda i: (i, 0)),
        out_shape=jax.ShapeDtypeStruct((T, D), F32),
        compiler_params=_cparams(("parallel",)), name="final_norm")(h, g)


def kernel(x, positions, attn_norm, w_in, mla_q_norm, mla_w_uq, mla_kv_norm, mla_w_ukv, conv_w, gmlp_ln, gmlp_w_s, gmlp_b_s, nsa_pe_k, nsa_w1_k, nsa_w2_k, nsa_pe_v, nsa_w1_v, nsa_w2_v, group_norm, w_o, ffn_norm, dense_w_gate, dense_w_up, dense_w_down, moe_router, moe_w_gate, moe_w_up, moe_w_down, final_norm):
    B, S, D = x.shape
    T = B * S
    depth = w_in.shape[0]
    h = x.reshape(T, D)
    pos = positions.astype(F32).reshape(T, 1)

    w_in_p = prep_w_in(w_in)
    wqn, wukt, wra, wrb, wuvp, freq, sign = prep_mla(mla_w_uq, mla_w_ukv)
    conv_w8 = jnp.pad(conv_w, ((0, 0), (0, 8 - CONV_K), (0, 0)))
    b_exp = jnp.repeat(jnp.swapaxes(gmlp_b_s, 1, 2), GROUP_W // GMLP_GROUPS, axis=2)
    sel, place = _head_select(NSA_HEADS)
    gx = _gate_expand()
    msel = cmp_to_sel(S)
    w_o_b = w_o.astype(BF16)
    wr_p = jnp.pad(moe_router, ((0, 0), (0, 0), (0, LANES - N_EXPERTS)))

    for l in range(depth):
        x_mla, x_conv, x_gmlp, x_nq, x_cmp, x_sel, x_win, x_gate = in_proj(h, attn_norm[l][None, :], w_in_p[l])

        q, k = mla_proj(x_mla, pos, mla_q_norm[l][None, :], mla_kv_norm[l][None, :],
                        wqn[l], wukt[l], wra[l], wrb[l], freq, sign)
        y_mla = mla_attn(q, k, wuvp[l], B, S)

        y_cg = conv_gmlp(x_conv, x_gmlp, conv_w8[l], gmlp_ln[l][None, :], gmlp_w_s[l], b_exp[l], S)

        pea, peb, w1a, w1b, w2 = prep_compress(nsa_pe_k[l], nsa_w1_k[l], nsa_w2_k[l],
                                               nsa_pe_v[l], nsa_w1_v[l], nsa_w2_v[l])
        z2 = x_cmp.reshape(B, S // CMP_STRIDE, CMP_STRIDE * LANES)
        kcv = compress(z2, pea, peb, w1a, w1b, w2)
        o_cmp, sel_bias = nsa_cmp(x_nq, kcv, msel, sel, place, B, S)
        y_nsa = nsa_attn(x_nq, x_sel, x_win, sel_bias, o_cmp, x_gate, sel, place, gx, B, S)

        h = mix_out(y_mla, y_cg, y_nsa, h, group_norm[l].reshape(1, D), w_o_b[l])

        fg = ffn_norm[l][None, :]
        if l % 2 == 0:
            h = ffn_dense(h, fg, dense_w_gate[l // 2].astype(BF16), dense_w_up[l // 2].astype(BF16),
                          dense_w_down[l // 2].astype(BF16))
        else:
            h = moe_ffn(h, fg, wr_p[l // 2], moe_w_gate[l // 2].astype(BF16), moe_w_up[l // 2].astype(BF16),
                        moe_w_down[l // 2].astype(BF16))
    return final_rms(h, final_norm[None, :]).reshape(B, S, D)
```

```python
import functools

import numpy as np
import jax
import jax.numpy as jnp
from jax import lax
from jax.experimental import pallas as pl
from jax.experimental.pallas import tpu as pltpu

F32 = jnp.float32
BF16 = jnp.bfloat16

D_MODEL = 1024
GROUP_W = 256
MLA_HEADS = 4
MLA_NOPE = 64
MLA_ROPE = 32
MLA_V = 64
MLA_Q_RANK = 256
MLA_KV_RANK = 128
ROPE_THETA = 10000.0
CONV_K = 3
GMLP_GROUPS = 4
GMLP_CHUNK = 128
NSA_HEADS = 4
NSA_DH = 64
CMP_LEN = 32
CMP_STRIDE = 16
CMP_HIDDEN = 256
SEL_LEN = 64
N_SELECT = 16
WINDOW = 512
FORCE_SCORE = 1e4
D_FF = 3584
N_EXPERTS = 8
EPS = 1e-6
NEG = -1e30
LOG2E = 1.4426950408889634

LANES = 128
VMEM_LIMIT = 56 * 1024 * 1024

_IN_SIZES = (256, 128, 32, 256, 256, 256, 256, 256, 256, 64, 64, 64, 64, 64, 64, 12)
_IN_OFF = np.concatenate([[0], np.cumsum(_IN_SIZES)]).tolist()

_W_MLA, _W_CONV, _W_GMLP, _W_NQ, _W_KV, _W_GATE = 640, 768, 512, 256, 128, 128
_IN_SPLITS = (_W_MLA, _W_CONV, _W_GMLP, _W_NQ, _W_KV, _W_KV, _W_KV, _W_GATE)
_D_IN_P = sum(_IN_SPLITS)

_NT = (((1,), (1,)), ((), ()))


def _cparams(sem, vmem=VMEM_LIMIT):
    return pltpu.CompilerParams(dimension_semantics=sem, vmem_limit_bytes=vmem)


def _rms(x, g):
    return x * lax.rsqrt(jnp.mean(x * x, axis=-1, keepdims=True) + EPS) * g


def _dot(a, b):
    return jnp.dot(a, b, preferred_element_type=F32)


def _dot_exact(a, b):
    return jnp.dot(a, b, preferred_element_type=F32, precision=lax.Precision.HIGHEST)


def _gelu(x):
    return 0.5 * x * (1.0 + jnp.tanh(0.7978845608028654 * (x + 0.044715 * (x * x * x))))


def _in_proj_kernel(h_ref, g_ref, w_ref, *o_refs):
    xn = _rms(h_ref[...], g_ref[...]).astype(BF16)
    r = _dot(xn, w_ref[...]).astype(BF16)
    off = 0
    for o_ref, w in zip(o_refs, _IN_SPLITS):
        o_ref[...] = r[:, off:off + w]
        off += w


def in_proj(h, g, w):
    T, D = h.shape
    tm = min(512, T)
    return pl.pallas_call(
        _in_proj_kernel, grid=(T // tm,),
        in_specs=[pl.BlockSpec((tm, D), lambda i: (i, 0)),
                  pl.BlockSpec((1, D), lambda i: (0, 0)),
                  pl.BlockSpec((D, _D_IN_P), lambda i: (0, 0))],
        out_specs=[pl.BlockSpec((tm, w_), lambda i: (i, 0)) for w_ in _IN_SPLITS],
        out_shape=[jax.ShapeDtypeStruct((T, w_), BF16) for w_ in _IN_SPLITS],
        compiler_params=_cparams(("parallel",)), name="in_proj")(h, g, w)


def prep_w_in(w_in):
    o = _IN_OFF
    L, D, _ = w_in.shape
    z = lambda n: jnp.zeros((L, D, n), F32)
    kr = w_in[:, :, o[2]:o[3]]
    half = MLA_ROPE // 2
    kr_sw = jnp.concatenate([kr[..., half:], kr[..., :half]], axis=-1)
    cols = [w_in[:, :, o[0]:o[2]],
            kr, z(96), kr_sw, z(96),
            w_in[:, :, o[3]:o[6]],
            w_in[:, :, o[6]:o[8]],
            w_in[:, :, o[8]:o[9]] * (NSA_DH ** -0.5 * LOG2E),
            w_in[:, :, o[9]:o[15]],
            w_in[:, :, o[15]:o[16]], z(_W_GATE - 12)]
    return jnp.concatenate(cols, axis=-1).astype(BF16)


_MLA_QW = 2 * LANES


def _mla_proj_kernel(x_ref, pos_ref, qg_ref, kvg_ref, wqn_ref, wukt_ref, wra_ref, wrb_ref, freq_ref, sign_ref,
                     q_ref, k_ref):
    x = x_ref[...].astype(F32)
    cq, ckv = x[:, :256], x[:, 256:384]
    kra, krb = x[:, 384:512], x[:, 512:640]
    ang = pos_ref[...] * freq_ref[...]
    c = jnp.cos(ang)
    s = jnp.sin(ang) * sign_ref[...]
    cqn = _rms(cq, qg_ref[...]).astype(BF16)
    qn = _dot(cqn, wqn_ref[...])
    ra = _dot(cqn, wra_ref[...])
    rb = _dot(cqn, wrb_ref[...])
    for h in range(MLA_HEADS):
        hs = slice(h * LANES, (h + 1) * LANES)
        q_ref[:, h * _MLA_QW:h * _MLA_QW + LANES] = _dot(qn[:, hs].astype(BF16), wukt_ref[h]).astype(BF16)
        q_ref[:, h * _MLA_QW + LANES:(h + 1) * _MLA_QW] = (ra[:, hs] * c + rb[:, hs] * s).astype(BF16)
    k_ref[:, :LANES] = _rms(ckv, kvg_ref[...]).astype(BF16)
    k_ref[:, LANES:] = (kra * c + krb * s).astype(BF16)


def mla_proj(x_mla, pos, qg, kvg, wqn, wukt, wra, wrb, freq, sign):
    T = x_mla.shape[0]
    tm = min(512, T)
    full = lambda a: pl.BlockSpec(a.shape, lambda i: (0,) * a.ndim)
    return pl.pallas_call(
        _mla_proj_kernel, grid=(T // tm,),
        in_specs=[pl.BlockSpec((tm, _W_MLA), lambda i: (i, 0)),
                  pl.BlockSpec((tm, 1), lambda i: (i, 0)),
                  full(qg), full(kvg), full(wqn), full(wukt), full(wra), full(wrb), full(freq), full(sign)],
        out_specs=[pl.BlockSpec((tm, MLA_HEADS * _MLA_QW), lambda i: (i, 0)),
                   pl.BlockSpec((tm, _MLA_QW), lambda i: (i, 0))],
        out_shape=[jax.ShapeDtypeStruct((T, MLA_HEADS * _MLA_QW), BF16),
                   jax.ShapeDtypeStruct((T, _MLA_QW), BF16)],
        compiler_params=_cparams(("parallel",)), name="mla_proj")(
            x_mla, pos, qg, kvg, wqn, wukt, wra, wrb, freq, sign)


def prep_mla(w_uq, w_ukv):
    L = w_uq.shape[0]
    scale = (MLA_NOPE + MLA_ROPE) ** -0.5 * LOG2E
    half = MLA_ROPE // 2
    hq = MLA_NOPE + MLA_ROPE
    hk = MLA_NOPE + MLA_V
    zq = lambda n: jnp.zeros((L, MLA_Q_RANK, n), F32)
    qn, ra, rb, ukt, uvp = [], [], [], [], []
    for h in range(MLA_HEADS):
        rp = w_uq[:, :, h * hq + MLA_NOPE:(h + 1) * hq]
        qn += [w_uq[:, :, h * hq:h * hq + MLA_NOPE], zq(LANES - MLA_NOPE)]
        ra += [rp, zq(LANES - MLA_ROPE)]
        rb += [jnp.concatenate([rp[..., half:], rp[..., :half]], axis=-1), zq(LANES - MLA_ROPE)]
        w_uk = w_ukv[:, :, h * hk:h * hk + MLA_NOPE]
        ukt.append(jnp.pad(jnp.swapaxes(w_uk, 1, 2), ((0, 0), (0, LANES - MLA_NOPE), (0, 0))))
        w_uv = w_ukv[:, :, h * hk + MLA_NOPE:(h + 1) * hk]
        uvp.append(jnp.pad(w_uv, ((0, 0), (0, 0), (h * MLA_V, (MLA_HEADS - 1 - h) * MLA_V))))
    wqn = (jnp.concatenate(qn, axis=-1) * scale).astype(BF16)
    wra = (jnp.concatenate(ra, axis=-1) * scale).astype(BF16)
    wrb = (jnp.concatenate(rb, axis=-1) * scale).astype(BF16)
    wukt = jnp.stack(ukt, axis=1).astype(BF16)
    wuvp = jnp.stack(uvp, axis=1).astype(BF16)
    inv = ROPE_THETA ** (-jnp.arange(half, dtype=F32) / half)
    pad = jnp.zeros((LANES - MLA_ROPE,), F32)
    freq = jnp.concatenate([inv, inv, pad])[None, :]
    sign = jnp.concatenate([-jnp.ones((half,), F32), jnp.ones((half,), F32), pad])[None, :]
    return wqn, wukt, wra, wrb, wuvp, freq, sign


def _causal_mask(t):
    row = lax.broadcasted_iota(jnp.int32, (t, t), 0)
    col = lax.broadcasted_iota(jnp.int32, (t, t), 1)
    return row >= col


def _lane_fold(x, op):
    r = x[:, :LANES]
    for c in range(1, x.shape[1] // LANES):
        r = op(r, x[:, c * LANES:(c + 1) * LANES])
    return r


class _Softmax:
    def __init__(self, q, m_ref, l_ref, acc_ref):
        self.q, self.m_ref, self.l_ref, self.acc_ref = q, m_ref, l_ref, acc_ref
        m_ref[...] = jnp.full(m_ref.shape, NEG, F32)
        l_ref[...] = jnp.zeros(l_ref.shape, F32)
        acc_ref[...] = jnp.zeros(acc_ref.shape, F32)

    def tile(self, k, v, mask_fn=None):
        s = lax.dot_general(self.q, k, _NT, preferred_element_type=F32)
        if mask_fn is not None:
            s = mask_fn(s)
        m_old = self.m_ref[...]
        m_new = jnp.maximum(m_old, jnp.max(_lane_fold(s, jnp.maximum), axis=-1, keepdims=True))
        alpha = jnp.exp2(m_old - m_new)
        parts = [jnp.exp2(s[:, c * LANES:(c + 1) * LANES] - m_new) for c in range(s.shape[1] // LANES)]
        lsum = parts[0]
        for p in parts[1:]:
            lsum = lsum + p
        self.l_ref[...] = alpha * self.l_ref[...] + lsum
        p = jnp.concatenate(parts, axis=1).astype(BF16)
        self.acc_ref[...] = alpha * self.acc_ref[...] + _dot(p, v)
        self.m_ref[...] = m_new

    def result(self):
        return self.acc_ref[...] / jnp.sum(self.l_ref[...], axis=-1, keepdims=True)


def _local_rc(nh, tq):
    row = lax.broadcasted_iota(jnp.int32, (nh * tq, tq), 0) & (tq - 1)
    col = lax.broadcasted_iota(jnp.int32, (nh * tq, tq), 1)
    return row, col


def _mask_last_block(s, keep, tq):
    w = s.shape[1]
    last = jnp.where(keep, s[:, w - tq:], NEG)
    return last if w == tq else jnp.concatenate([s[:, :w - tq], last], axis=1)


def _causal_sweep(sm, i, nh, tq, tk, load):
    n_full = (i * tq) // tk

    def body(j, c):
        sm.tile(*load(pl.multiple_of(j * tk, tk), tk))
        return c

    lax.fori_loop(0, n_full, body, 0)
    row, col = _local_rc(nh, tq)
    per = tk // tq
    for r in range(per):
        @pl.when(i % per == r)
        def _():
            k, v = load(pl.multiple_of(n_full * tk, tk), (r + 1) * tq)
            sm.tile(k, v, lambda s: _mask_last_block(s, col <= row, tq))


def _mla_attn_kernel(q_ref, k_ref, wuvp_ref, o_ref, qs_ref, m_ref, l_ref, acc_ref, *, tq, tk):
    i = pl.program_id(1)
    for h in range(MLA_HEADS):
        qs_ref[h * tq:(h + 1) * tq, :] = q_ref[:, h * _MLA_QW:(h + 1) * _MLA_QW]
    sm = _Softmax(qs_ref[...], m_ref, l_ref, acc_ref)

    def load(start, width):
        k = k_ref[pl.ds(start, width), :]
        return k, k[:, :LANES]

    _causal_sweep(sm, i, MLA_HEADS, tq, tk, load)
    o_lat = sm.result().astype(BF16)
    out = _dot(o_lat[:tq], wuvp_ref[0])
    for h in range(1, MLA_HEADS):
        out = out + _dot(o_lat[h * tq:(h + 1) * tq], wuvp_ref[h])
    o_ref[...] = out.astype(BF16)


_ATT_TQ, _ATT_TK = 256, 1024


def mla_attn(q, k, wuvp, B, S):
    tq, tk = min(_ATT_TQ, S), min(_ATT_TK, S)
    assert tq & (tq - 1) == 0
    M = MLA_HEADS * tq
    q3, k3 = q.reshape(B, S, MLA_HEADS * _MLA_QW), k.reshape(B, S, _MLA_QW)
    out = pl.pallas_call(
        functools.partial(_mla_attn_kernel, tq=tq, tk=tk), grid=(B, S // tq),
        in_specs=[pl.BlockSpec((None, tq, MLA_HEADS * _MLA_QW), lambda b, i: (b, i, 0)),
                  pl.BlockSpec((None, S, _MLA_QW), lambda b, i: (b, 0, 0)),
                  pl.BlockSpec(wuvp.shape, lambda b, i: (0, 0, 0))],
        out_specs=pl.BlockSpec((None, tq, 256), lambda b, i: (b, i, 0)),
        out_shape=jax.ShapeDtypeStruct((B, S, 256), BF16),
        scratch_shapes=[pltpu.VMEM((M, _MLA_QW), BF16), pltpu.VMEM((M, LANES), F32),
                        pltpu.VMEM((M, LANES), F32), pltpu.VMEM((M, LANES), F32)],
        compiler_params=_cparams(("parallel", "arbitrary")), name="mla_attn")(q3, k3, wuvp)
    return out.reshape(B * S, 256)


def _conv_gmlp_kernel(xc_ref, halo_ref, xg_ref, cw_ref, ln_ref, ws_ref, bs_ref, o_ref, *, tiles_per_seq):
    i = pl.program_id(0)
    tm = xc_ref.shape[0]
    xc = xc_ref[...].astype(F32)
    cb, u = xc[:, :256], xc[:, 256:512] * xc[:, 512:768]
    hx = halo_ref[...].astype(F32)
    keep = jnp.where(i % tiles_per_seq == 0, 0.0, 1.0)
    hu = hx[:, 256:512] * hx[:, 512:768] * keep
    p1, p2 = hu[7:8, :], hu[6:7, :]
    row = lax.broadcasted_iota(jnp.int32, (tm, 256), 0)
    u1 = jnp.where(row == 0, p1, pltpu.roll(u, 1, axis=0))
    u2 = jnp.where(row == 0, p2, jnp.where(row == 1, p1, pltpu.roll(u, 2, axis=0)))
    cw = cw_ref[...]
    y_conv = cb * (cw[0:1, :] * u2 + cw[1:2, :] * u1 + cw[2:3, :] * u)
    o_ref[:, :256] = y_conv.astype(BF16)

    xg = xg_ref[...].astype(F32)
    gu = _gelu(xg[:, :256])
    gv = _gelu(xg[:, 256:512])
    mu = jnp.mean(gv, axis=-1, keepdims=True)
    var = jnp.mean(jnp.square(gv - mu), axis=-1, keepdims=True)
    vn = (gv - mu) * lax.rsqrt(var + EPS) * ln_ref[...]
    lane_g = lax.broadcasted_iota(jnp.int32, (GMLP_CHUNK, 256), 1) // (256 // GMLP_GROUPS)
    tril = _causal_mask(GMLP_CHUNK)
    for c in range(tm // GMLP_CHUNK):
        vc = vn[c * GMLP_CHUNK:(c + 1) * GMLP_CHUNK, :]
        sv = bs_ref[...]
        for g in range(GMLP_GROUPS):
            wg = jnp.where(tril, ws_ref[g], 0.0).astype(BF16)
            sv = sv + _dot(wg, jnp.where(lane_g == g, vc, 0.0).astype(BF16))
        o_ref[c * GMLP_CHUNK:(c + 1) * GMLP_CHUNK, 256:512] = (
            gu[c * GMLP_CHUNK:(c + 1) * GMLP_CHUNK, :] * sv).astype(BF16)


def conv_gmlp(x_conv, x_gmlp, conv_w8, ln_g, w_s, b_exp, S):
    T = x_conv.shape[0]
    tm = min(512, S)
    full = lambda a: pl.BlockSpec(a.shape, lambda i: (0,) * a.ndim)
    return pl.pallas_call(
        functools.partial(_conv_gmlp_kernel, tiles_per_seq=S // tm), grid=(T // tm,),
        in_specs=[pl.BlockSpec((tm, _W_CONV), lambda i: (i, 0)),
                  pl.BlockSpec((8, _W_CONV), lambda i: (jnp.maximum(i * (tm // 8) - 1, 0), 0)),
                  pl.BlockSpec((tm, _W_GMLP), lambda i: (i, 0)),
                  full(conv_w8), full(ln_g), full(w_s), full(b_exp)],
        out_specs=pl.BlockSpec((tm, 512), lambda i: (i, 0)),
        out_shape=jax.ShapeDtypeStruct((T, 512), BF16),
        compiler_params=_cparams(("parallel",)), name="conv_gmlp")(
            x_conv, x_conv, x_gmlp, conv_w8, ln_g, w_s, b_exp)


def _compress_kernel(z_ref, pea_ref, peb_ref, w1a_ref, w1b_ref, w2_ref, o_ref):
    z = z_ref[...]
    n = z.shape[0]
    a = _dot(z, w1a_ref[...])
    b = _dot(z, w1b_ref[...])
    c = _dot_exact(pea_ref[...], w1a_ref[...].astype(F32)) + _dot_exact(peb_ref[...], w1b_ref[...].astype(F32))
    pre = a + pltpu.roll(b, n - 1, axis=0) + c[0:1, :]
    o_ref[...] = _dot(_gelu(pre).astype(BF16), w2_ref[...]).astype(BF16)


def compress(z2, pea, peb, w1a, w1b, w2):
    B, n, K = z2.shape
    full = lambda a: pl.BlockSpec(a.shape, lambda b: (0,) * a.ndim)
    return pl.pallas_call(
        _compress_kernel, grid=(B,),
        in_specs=[pl.BlockSpec((None, n, K), lambda b: (b, 0, 0)),
                  full(pea), full(peb), full(w1a), full(w1b), full(w2)],
        out_specs=pl.BlockSpec((None, n, LANES), lambda b: (b, 0, 0)),
        out_shape=jax.ShapeDtypeStruct((B, n, LANES), BF16),
        compiler_params=_cparams(("parallel",)), name="nsa_compress")(z2, pea, peb, w1a, w1b, w2)


def prep_compress(pe_k, w1_k, w2_k, pe_v, w1_v, w2_v):
    half = CMP_LEN // 2

    def w1_half(w1k, w1v):
        wk = w1k.reshape(half, NSA_DH, CMP_HIDDEN)
        wv = w1v.reshape(half, NSA_DH, CMP_HIDDEN)
        zk = jnp.zeros_like(wk)
        top = jnp.concatenate([wk, zk], axis=-1)
        bot = jnp.concatenate([zk, wv], axis=-1)
        return jnp.concatenate([top, bot], axis=1).reshape(half * 2 * NSA_DH, 2 * CMP_HIDDEN)

    n1 = half * NSA_DH
    w1a = w1_half(w1_k[:n1], w1_v[:n1]).astype(BF16)
    w1b = w1_half(w1_k[n1:], w1_v[n1:]).astype(BF16)

    def pe_half(pk, pv):
        row = jnp.concatenate([pk, pv], axis=-1).reshape(1, half * 2 * NSA_DH)
        return jnp.broadcast_to(row, (8, half * 2 * NSA_DH))

    pea = pe_half(pe_k[:half], pe_v[:half])
    peb = pe_half(pe_k[half:], pe_v[half:])
    z = jnp.zeros((CMP_HIDDEN, NSA_DH), F32)
    w2 = jnp.concatenate([jnp.concatenate([w2_k, z], axis=1),
                          jnp.concatenate([z, w2_v], axis=1)], axis=0).astype(BF16)
    return pea, peb, w1a, w1b, w2


def _head_select(nh):
    sel = np.zeros((nh, nh * NSA_DH, LANES), np.float32)
    place = np.zeros((nh, LANES, nh * NSA_DH), np.float32)
    for h in range(nh):
        for d in range(NSA_DH):
            sel[h, h * NSA_DH + d, d] = 1.0
            place[h, NSA_DH + d, h * NSA_DH + d] = 1.0
    return jnp.asarray(sel, BF16), jnp.asarray(place, BF16)


def _stack_heads(q, sel_ref, qs_ref):
    for h in range(NSA_HEADS):
        qs_ref[h] = _dot(q, sel_ref[h]).astype(BF16)


def _place_heads(o_list, place_ref):
    out = _dot(o_list[0].astype(BF16), place_ref[0])
    for h in range(1, NSA_HEADS):
        out = out + _dot(o_list[h].astype(BF16), place_ref[h])
    return out


def _gate(gl, h, branch):
    c = h * 3 + branch
    return 1.0 / (1.0 + jnp.exp(-gl[:, c:c + 1]))


def _nsa_cmp_kernel(q_ref, kv_ref, gl_ref, msel_ref, sel_ref, place_ref, o_ref, bias_ref, qs_ref, *, t, ns):
    i = pl.program_id(1)
    nc = kv_ref.shape[0]
    _stack_heads(q_ref[...], sel_ref, qs_ref)
    kv = kv_ref[...]
    gl = gl_ref[...].astype(F32)
    tpos = i * t + lax.broadcasted_iota(jnp.int32, (t, nc), 0)
    blk_end = lax.broadcasted_iota(jnp.int32, (t, nc), 1) * CMP_STRIDE + (CMP_LEN - 1)
    cmask = blk_end <= tpos
    p_sum = jnp.zeros((t, nc), F32)
    outs = []
    for h in range(NSA_HEADS):
        s = lax.dot_general(qs_ref[h], kv, _NT, preferred_element_type=F32)
        s = jnp.where(cmask, s, NEG)
        m = jnp.max(s, axis=-1, keepdims=True)
        e = jnp.where(cmask, jnp.exp2(s - m), 0.0)
        l = jnp.sum(e, axis=-1, keepdims=True)
        p = e / jnp.where(l > 0.0, l, 1.0)
        p_sum = p_sum + p
        outs.append(_gate(gl, h, 0) * _dot(p.astype(BF16), kv))
    o_ref[...] = _place_heads(outs, place_ref).astype(BF16)

    p_sel = _dot_exact(p_sum, msel_ref[...])
    blk = lax.broadcasted_iota(jnp.int32, (t, LANES), 1) - NSA_DH
    blk_f = blk.astype(F32)
    cur = (i * t + lax.broadcasted_iota(jnp.int32, (t, LANES), 0)) // SEL_LEN
    forced = (blk == 0) | (blk == cur) | (blk == cur - 1)
    v = jnp.where(forced, FORCE_SCORE, jnp.where(blk <= cur, p_sel, -1.0))
    v = jnp.where((blk >= 0) & (blk < ns), v, -jnp.inf)
    chosen = blk < 0
    for _ in range(min(N_SELECT, ns)):
        mx = jnp.max(v, axis=-1, keepdims=True)
        idx = jnp.min(jnp.where(v == mx, blk_f, float(LANES)), axis=-1, keepdims=True)
        hit = blk_f == idx
        chosen = chosen | hit
        v = jnp.where(hit, -jnp.inf, v)
    bias_ref[...] = jnp.where(chosen, 0.0, NEG).astype(BF16)


def nsa_cmp(q, kcv, gl, msel, sel, place, B, S):
    t = min(1024, S)
    nc = kcv.shape[1]
    ns = S // SEL_LEN
    full = lambda a: pl.BlockSpec(a.shape, lambda b, i: (0,) * a.ndim)
    return pl.pallas_call(
        functools.partial(_nsa_cmp_kernel, t=t, ns=ns), grid=(B, S // t),
        in_specs=[pl.BlockSpec((None, t, 256), lambda b, i: (b, i, 0)),
                  pl.BlockSpec((None, nc, LANES), lambda b, i: (b, 0, 0)),
                  pl.BlockSpec((None, t, LANES), lambda b, i: (b, i, 0)),
                  full(msel), full(sel), full(place)],
        out_specs=[pl.BlockSpec((None, t, 256), lambda b, i: (b, i, 0)),
                   pl.BlockSpec((None, t, LANES), lambda b, i: (b, i, 0))],
        out_shape=[jax.ShapeDtypeStruct((B, S, 256), BF16), jax.ShapeDtypeStruct((B, S, LANES), BF16)],
        scratch_shapes=[pltpu.VMEM((NSA_HEADS, t, LANES), BF16)],
        compiler_params=_cparams(("parallel", "arbitrary")), name="nsa_cmp")(
            q.reshape(B, S, 256), kcv, gl.reshape(B, S, LANES), msel, sel, place)


def cmp_to_sel(S):
    nc = S // CMP_STRIDE - CMP_LEN // CMP_STRIDE + 1
    ns = S // SEL_LEN
    cs = np.arange(nc)[:, None] * CMP_STRIDE
    ss = np.arange(ns)[None, :] * SEL_LEN
    ov = np.clip(np.minimum(cs + CMP_LEN, ss + SEL_LEN) - np.maximum(cs, ss), 0, None)
    assert ns <= LANES - NSA_DH, "selection blocks must fit the lanes beside one head's query"
    m = np.zeros((S // CMP_STRIDE, LANES), np.float32)
    m[:nc, NSA_DH:NSA_DH + ns] = ov.astype(np.float32) / np.float32(CMP_LEN)
    return jnp.asarray(m)


def _nsa_attn_kernel(q_ref, ksel_ref, kwin_ref, bias_ref, ocmp_ref, gl_ref, sel_ref, place_ref,
                     o_ref, kx_ref, kwp_ref, qs_ref, qw_ref, m_ref, l_ref, acc_ref, *, tq, tk):
    i = pl.program_id(1)
    S = ksel_ref.shape[0]

    @pl.when(i == 0)
    def _():
        r = lax.broadcasted_iota(jnp.int32, (S, LANES), 0)
        c = lax.broadcasted_iota(jnp.int32, (S, LANES), 1)
        onehot = jnp.where(r // SEL_LEN == c - NSA_DH, 1.0, 0.0)
        kx_ref[...] = jnp.where(c < NSA_DH, ksel_ref[...].astype(F32), onehot).astype(BF16)
        kwp_ref[:WINDOW, :] = jnp.zeros((WINDOW, LANES), BF16)
        kwp_ref[WINDOW:, :] = kwin_ref[...]

    q = q_ref[...]
    bias = bias_ref[...].astype(F32)
    for h in range(NSA_HEADS):
        qs = _dot(q, sel_ref[h])
        qw_ref[h * tq:(h + 1) * tq, :] = qs.astype(BF16)
        qs_ref[h * tq:(h + 1) * tq, :] = (qs + bias).astype(BF16)

    def heads(x):
        return [x[h * tq:(h + 1) * tq] for h in range(NSA_HEADS)]

    sm = _Softmax(qs_ref[...], m_ref, l_ref, acc_ref)
    _causal_sweep(sm, i, NSA_HEADS, tq, tk,
                  lambda start, width: (kx_ref[pl.ds(start, width), :], ksel_ref[pl.ds(start, width), :]))
    o_sel = heads(sm.result())

    sm = _Softmax(qw_ref[...], m_ref, l_ref, acc_ref)
    row, col = _local_rc(NSA_HEADS, tq)
    nblk = WINDOW // tq + 1

    def win_mask(s):
        blocks = []
        for c in range(nblk):
            sc = s[:, c * tq:(c + 1) * tq]
            if c == 0:
                sc = jnp.where(row < col, sc, NEG)
            if c == nblk - 1:
                sc = jnp.where(col <= row, sc, NEG)
            else:
                sc = sc + jnp.where(i >= nblk - 1 - c, 0.0, NEG)
            blocks.append(sc)
        return jnp.concatenate(blocks, axis=1)

    kw = kwp_ref[pl.ds(pl.multiple_of(i * tq, tq), WINDOW + tq), :]
    sm.tile(kw, kw, win_mask)
    o_win = heads(sm.result())

    gl = gl_ref[...].astype(F32)
    gated = [_gate(gl, h, 1) * o_sel[h] + _gate(gl, h, 2) * o_win[h] for h in range(NSA_HEADS)]
    o_ref[...] = (ocmp_ref[...].astype(F32) + _place_heads(gated, place_ref)).astype(BF16)


def nsa_attn(q, ksel, kwin, bias, o_cmp, gl, sel, place, B, S):
    tq, tk = min(_ATT_TQ, S), min(_ATT_TK, S)
    assert WINDOW % tq == 0
    M = NSA_HEADS * tq
    full = lambda a: pl.BlockSpec(a.shape, lambda b, i: (0,) * a.ndim)
    tile2 = lambda w: pl.BlockSpec((None, tq, w), lambda b, i: (b, i, 0))
    seq = pl.BlockSpec((None, S, LANES), lambda b, i: (b, 0, 0))
    out = pl.pallas_call(
        functools.partial(_nsa_attn_kernel, tq=tq, tk=tk), grid=(B, S // tq),
        in_specs=[tile2(256), seq, seq, tile2(LANES), tile2(256), tile2(LANES), full(sel), full(place)],
        out_specs=tile2(256),
        out_shape=jax.ShapeDtypeStruct((B, S, 256), BF16),
        scratch_shapes=[pltpu.VMEM((S, LANES), BF16), pltpu.VMEM((S + WINDOW, LANES), BF16),
                        pltpu.VMEM((M, LANES), BF16), pltpu.VMEM((M, LANES), BF16),
                        pltpu.VMEM((M, LANES), F32), pltpu.VMEM((M, LANES), F32), pltpu.VMEM((M, LANES), F32)],
        compiler_params=_cparams(("parallel", "arbitrary")), name="nsa_attn")(
            q.reshape(B, S, 256), ksel.reshape(B, S, LANES), kwin.reshape(B, S, LANES), bias, o_cmp,
            gl.reshape(B, S, LANES), sel, place)
    return out.reshape(B * S, 256)


def _mix_out_kernel(ymla_ref, ycg_ref, ynsa_ref, h_ref, gn_ref, wo_ref, o_ref):
    gn = gn_ref[...]
    parts = [ymla_ref[...].astype(F32), ycg_ref[:, :256].astype(F32), ycg_ref[:, 256:].astype(F32),
             ynsa_ref[...].astype(F32)]
    yn = [_rms(p, gn[:, k * GROUP_W:(k + 1) * GROUP_W]).astype(BF16) for k, p in enumerate(parts)]
    o_ref[...] = h_ref[...] + _dot(jnp.concatenate(yn, axis=1), wo_ref[...])


def mix_out(y_mla, y_cg, y_nsa, h, gn, wo):
    T, D = h.shape
    tm = min(512, T)
    row = lambda w: pl.BlockSpec((tm, w), lambda i: (i, 0))
    return pl.pallas_call(
        _mix_out_kernel, grid=(T // tm,),
        in_specs=[row(256), row(512), row(256), row(D),
                  pl.BlockSpec((1, D), lambda i: (0, 0)), pl.BlockSpec((D, D), lambda i: (0, 0))],
        out_specs=row(D), out_shape=jax.ShapeDtypeStruct((T, D), F32),
        compiler_params=_cparams(("parallel",)), name="mix_out")(y_mla, y_cg, y_nsa, h, gn, wo)


def _ffn_step(j, nj, x_ref, g_ref, wg_ref, wu_ref, wd_ref, o_ref, hn_ref, acc_ref, residual):
    @pl.when(j == 0)
    def _():
        hn_ref[...] = _rms(x_ref[...], g_ref[...]).astype(BF16)
        acc_ref[...] = jnp.zeros(acc_ref.shape, F32)

    hn = hn_ref[...]
    a = _dot(hn, wg_ref[...])
    b = _dot(hn, wu_ref[...])
    act = (a / (1.0 + jnp.exp(-a)) * b).astype(BF16)
    acc_ref[...] += _dot(act, wd_ref[...])

    @pl.when(j == nj - 1)
    def _():
        o_ref[...] = (x_ref[...] + acc_ref[...]) if residual else acc_ref[...]


def _ffn_kernel(x_ref, g_ref, wg_ref, wu_ref, wd_ref, o_ref, hn_ref, acc_ref):
    _ffn_step(pl.program_id(1), pl.num_programs(1), x_ref, g_ref, wg_ref, wu_ref, wd_ref, o_ref, hn_ref,
              acc_ref, True)


_FFN_TF = 512


def ffn_dense(h, g, wg, wu, wd):
    T, D = h.shape
    F = wg.shape[1]
    tm = min(1024, T)
    return pl.pallas_call(
        _ffn_kernel, grid=(T // tm, F // _FFN_TF),
        in_specs=[pl.BlockSpec((tm, D), lambda i, j: (i, 0)),
                  pl.BlockSpec((1, D), lambda i, j: (0, 0)),
                  pl.BlockSpec((D, _FFN_TF), lambda i, j: (0, j)),
                  pl.BlockSpec((D, _FFN_TF), lambda i, j: (0, j)),
                  pl.BlockSpec((_FFN_TF, D), lambda i, j: (j, 0))],
        out_specs=pl.BlockSpec((tm, D), lambda i, j: (i, 0)),
        out_shape=jax.ShapeDtypeStruct((T, D), F32),
        scratch_shapes=[pltpu.VMEM((tm, D), BF16), pltpu.VMEM((tm, D), F32)],
        compiler_params=_cparams(("parallel", "arbitrary")), name="ffn_dense")(h, g, wg, wu, wd)


def _ffn_grouped_kernel(te_ref, nu_ref, x_ref, g_ref, wg_ref, wu_ref, wd_ref, o_ref, hn_ref, acc_ref):
    del te_ref
    i, j, nj = pl.program_id(0), pl.program_id(1), pl.num_programs(1)
    used = i < nu_ref[0]

    @pl.when(used)
    def _():
        _ffn_step(j, nj, x_ref, g_ref, wg_ref, wu_ref, wd_ref, o_ref, hn_ref, acc_ref, False)

    @pl.when(jnp.logical_not(used))
    def _():
        o_ref[...] = jnp.zeros(o_ref.shape, F32)


def ffn_grouped(xs, g, wg, wu, wd, tile_expert, n_used, tm):
    NP, D = xs.shape
    F = wg.shape[2]
    nf = F // _FFN_TF

    def jj(i, j, nu):
        return jnp.where(i < nu[0], j, nf - 1)

    grid_spec = pltpu.PrefetchScalarGridSpec(
        num_scalar_prefetch=2, grid=(NP // tm, nf),
        in_specs=[pl.BlockSpec((tm, D), lambda i, j, te, nu: (i, 0)),
                  pl.BlockSpec((1, D), lambda i, j, te, nu: (0, 0)),
                  pl.BlockSpec((None, D, _FFN_TF), lambda i, j, te, nu: (te[i], 0, jj(i, j, nu))),
                  pl.BlockSpec((None, D, _FFN_TF), lambda i, j, te, nu: (te[i], 0, jj(i, j, nu))),
                  pl.BlockSpec((None, _FFN_TF, D), lambda i, j, te, nu: (te[i], jj(i, j, nu), 0))],
        out_specs=pl.BlockSpec((tm, D), lambda i, j, te, nu: (i, 0)),
        scratch_shapes=[pltpu.VMEM((tm, D), BF16), pltpu.VMEM((tm, D), F32)])
    return pl.pallas_call(
        _ffn_grouped_kernel, grid_spec=grid_spec,
        out_shape=jax.ShapeDtypeStruct((NP, D), F32),
        compiler_params=_cparams(("arbitrary", "arbitrary")), name="ffn_grouped")(
            tile_expert, n_used, xs, g, wg, wu, wd)


def _router_kernel(h_ref, g_ref, wr_ref, o_ref):
    hn = _rms(h_ref[...], g_ref[...])
    logits = _dot_exact(hn, wr_ref[...])
    tm = logits.shape[0]
    lane = lax.broadcasted_iota(jnp.int32, (tm, LANES), 1)
    lane_f = lane.astype(F32)
    lg = jnp.where(lane < N_EXPERTS, logits, -jnp.inf)
    m1 = jnp.max(lg, axis=-1, keepdims=True)
    i1 = jnp.min(jnp.where(lg == m1, lane_f, float(LANES)), axis=-1, keepdims=True)
    lg2 = jnp.where(lane_f == i1, -jnp.inf, lg)
    m2 = jnp.max(lg2, axis=-1, keepdims=True)
    i2 = jnp.min(jnp.where(lg2 == m2, lane_f, float(LANES)), axis=-1, keepdims=True)
    e = jnp.exp(m2 - m1)
    w1 = 1.0 / (1.0 + e)
    w2 = e / (1.0 + e)
    o_ref[...] = jnp.where(lane == 0, i1,
                           jnp.where(lane == 1, i2,
                                     jnp.where(lane == 2, w1, jnp.where(lane == 3, w2, 0.0))))


def router(h, g, wr):
    T, D = h.shape
    tm = min(512, T)
    return pl.pallas_call(
        _router_kernel, grid=(T // tm,),
        in_specs=[pl.BlockSpec((tm, D), lambda i: (i, 0)), pl.BlockSpec((1, D), lambda i: (0, 0)),
                  pl.BlockSpec((D, LANES), lambda i: (0, 0))],
        out_specs=pl.BlockSpec((tm, LANES), lambda i: (i, 0)),
        out_shape=jax.ShapeDtypeStruct((T, LANES), F32),
        compiler_params=_cparams(("parallel",)), name="moe_router")(h, g, wr)


def _row_copy(src, s, dst, d, sem):
    return pltpu.make_async_copy(src.at[pl.ds(s, 1)], dst.at[pl.ds(d, 1)], sem)


def _dispatch_kernel(dest_ref, h_ref, xs_in_ref, xs_ref, sem):
    del xs_in_ref
    tm = h_ref.shape[0]

    def issue(r, c):
        _row_copy(h_ref, r, xs_ref, dest_ref[0, 0, 2 * r], sem).start()
        _row_copy(h_ref, r, xs_ref, dest_ref[0, 0, 2 * r + 1], sem).start()
        return c

    lax.fori_loop(0, tm, issue, 0, unroll=8)

    def drain(r, c):
        _row_copy(h_ref, 0, xs_ref, 0, sem).wait()
        _row_copy(h_ref, 0, xs_ref, 0, sem).wait()
        return c

    lax.fori_loop(0, tm, drain, 0, unroll=8)


def dispatch(h, dest, xs_zero):
    T, D = h.shape
    tm = min(512, T)
    dest3 = dest.reshape(T // tm, 1, 2 * tm)
    return pl.pallas_call(
        _dispatch_kernel, grid=(T // tm,),
        in_specs=[pl.BlockSpec((1, 1, 2 * tm), lambda i: (i, 0, 0), memory_space=pltpu.SMEM),
                  pl.BlockSpec((tm, D), lambda i: (i, 0)),
                  pl.BlockSpec(memory_space=pl.ANY)],
        out_specs=pl.BlockSpec(memory_space=pl.ANY),
        out_shape=jax.ShapeDtypeStruct(xs_zero.shape, F32),
        scratch_shapes=[pltpu.SemaphoreType.DMA(())],
        input_output_aliases={2: 0},
        compiler_params=_cparams(("arbitrary",)), name="moe_dispatch")(dest3, h, xs_zero)


def _combine_kernel(dest_ref, h_ref, route_ref, ys_ref, o_ref, b0_ref, b1_ref, sem):
    tm = h_ref.shape[0]

    def issue(r, c):
        _row_copy(ys_ref, dest_ref[0, 0, 2 * r], b0_ref, r, sem).start()
        _row_copy(ys_ref, dest_ref[0, 0, 2 * r + 1], b1_ref, r, sem).start()
        return c

    lax.fori_loop(0, tm, issue, 0, unroll=8)

    def drain(r, c):
        _row_copy(ys_ref, 0, b0_ref, 0, sem).wait()
        _row_copy(ys_ref, 0, b1_ref, 0, sem).wait()
        return c

    lax.fori_loop(0, tm, drain, 0, unroll=8)
    route = route_ref[...]
    o_ref[...] = h_ref[...] + route[:, 2:3] * b0_ref[...] + route[:, 3:4] * b1_ref[...]


def combine(h, route, dest, ys):
    T, D = h.shape
    tm = min(256, T)
    dest3 = dest.reshape(T // tm, 1, 2 * tm)
    return pl.pallas_call(
        _combine_kernel, grid=(T // tm,),
        in_specs=[pl.BlockSpec((1, 1, 2 * tm), lambda i: (i, 0, 0), memory_space=pltpu.SMEM),
                  pl.BlockSpec((tm, D), lambda i: (i, 0)),
                  pl.BlockSpec((tm, LANES), lambda i: (i, 0)),
                  pl.BlockSpec(memory_space=pl.ANY)],
        out_specs=pl.BlockSpec((tm, D), lambda i: (i, 0)),
        out_shape=jax.ShapeDtypeStruct((T, D), F32),
        scratch_shapes=[pltpu.VMEM((tm, D), F32), pltpu.VMEM((tm, D), F32), pltpu.SemaphoreType.DMA(())],
        compiler_params=_cparams(("arbitrary",)), name="moe_combine")(dest3, h, route, ys)


def moe_ffn(h, g, wr, wg, wu, wd):
    T, D = h.shape
    tm = min(1024, T)
    route = router(h, g, wr)
    e_flat = route[:, :2].astype(jnp.int32).reshape(2 * T)
    onehot = (e_flat[:, None] == jnp.arange(N_EXPERTS, dtype=jnp.int32)[None, :]).astype(jnp.int32)
    csum = jnp.cumsum(onehot, axis=0)
    counts = csum[-1]
    rank = jnp.sum((csum - 1) * onehot, axis=1)
    tiles = (counts + tm - 1) // tm
    tile_end = jnp.cumsum(tiles)
    offs = (tile_end - tiles) * tm
    dest = (jnp.sum(onehot * offs[None, :], axis=1) + rank).astype(jnp.int32)
    n_tiles = (2 * T) // tm + N_EXPERTS
    n_used = tile_end[-1].astype(jnp.int32)
    tile_ids = jnp.minimum(jnp.arange(n_tiles, dtype=jnp.int32), n_used - 1)
    tile_expert = jnp.sum((tile_ids[:, None] >= tile_end[None, :]).astype(jnp.int32), axis=1).astype(jnp.int32)
    xs = dispatch(h, dest, jnp.zeros((n_tiles * tm, D), F32))
    ys = ffn_grouped(xs, g, wg, wu, wd, tile_expert, n_used.reshape(1), tm)
    return combine(h, route, dest, ys)


def _final_norm_kernel(h_ref, g_ref, o_ref):
    o_ref[...] = _rms(h_ref[...], g_ref[...])


def final_rms(h, g):
    T, D = h.shape
    tm = min(1024, T)
    return pl.pallas_call(
        _final_norm_kernel, grid=(T // tm,),
        in_specs=[pl.BlockSpec((tm, D), lambda i: (i, 0)), pl.BlockSpec((1, D), lambda i: (0, 0))],
        out_specs=pl.BlockSpec((tm, D), lambda i: (i, 0)),
        out_shape=jax.ShapeDtypeStruct((T, D), F32),
        compiler_params=_cparams(("parallel",)), name="final_norm")(h, g)


def kernel(x, positions, attn_norm, w_in, mla_q_norm, mla_w_uq, mla_kv_norm, mla_w_ukv, conv_w, gmlp_ln, gmlp_w_s, gmlp_b_s, nsa_pe_k, nsa_w1_k, nsa_w2_k, nsa_pe_v, nsa_w1_v, nsa_w2_v, group_norm, w_o, ffn_norm, dense_w_gate, dense_w_up, dense_w_down, moe_router, moe_w_gate, moe_w_up, moe_w_down, final_norm):
    B, S, D = x.shape
    T = B * S
    depth = w_in.shape[0]
    h = x.reshape(T, D)
    pos = positions.astype(F32).reshape(T, 1)

    w_in_p = prep_w_in(w_in)
    wqn, wukt, wra, wrb, wuvp, freq, sign = prep_mla(mla_w_uq, mla_w_ukv)
    conv_w8 = jnp.pad(conv_w, ((0, 0), (0, 8 - CONV_K), (0, 0)))
    b_exp = jnp.repeat(jnp.swapaxes(gmlp_b_s, 1, 2), GROUP_W // GMLP_GROUPS, axis=2)
    sel, place = _head_select(NSA_HEADS)
    msel = cmp_to_sel(S)
    w_o_b = w_o.astype(BF16)
    wr_p = jnp.pad(moe_router, ((0, 0), (0, 0), (0, LANES - N_EXPERTS)))

    for l in range(depth):
        x_mla, x_conv, x_gmlp, x_nq, x_cmp, x_sel, x_win, x_gate = in_proj(h, attn_norm[l][None, :], w_in_p[l])

        q, k = mla_proj(x_mla, pos, mla_q_norm[l][None, :], mla_kv_norm[l][None, :],
                        wqn[l], wukt[l], wra[l], wrb[l], freq, sign)
        y_mla = mla_attn(q, k, wuvp[l], B, S)

        y_cg = conv_gmlp(x_conv, x_gmlp, conv_w8[l], gmlp_ln[l][None, :], gmlp_w_s[l], b_exp[l], S)

        pea, peb, w1a, w1b, w2 = prep_compress(nsa_pe_k[l], nsa_w1_k[l], nsa_w2_k[l],
                                               nsa_pe_v[l], nsa_w1_v[l], nsa_w2_v[l])
        z2 = x_cmp.reshape(B, S // CMP_STRIDE, CMP_STRIDE * LANES)
        kcv = compress(z2, pea, peb, w1a, w1b, w2)
        o_cmp, sel_bias = nsa_cmp(x_nq, kcv, x_gate, msel, sel, place, B, S)
        y_nsa = nsa_attn(x_nq, x_sel, x_win, sel_bias, o_cmp, x_gate, sel, place, B, S)

        h = mix_out(y_mla, y_cg, y_nsa, h, group_norm[l].reshape(1, D), w_o_b[l])

        fg = ffn_norm[l][None, :]
        if l % 2 == 0:
            h = ffn_dense(h, fg, dense_w_gate[l // 2].astype(BF16), dense_w_up[l // 2].astype(BF16),
                          dense_w_down[l // 2].astype(BF16))
        else:
            h = moe_ffn(h, fg, wr_p[l // 2], moe_w_gate[l // 2].astype(BF16), moe_w_up[l // 2].astype(BF16),
                        moe_w_down[l // 2].astype(BF16))
    return final_rms(h, final_norm[None, :]).reshape(B, S, D)
```

```python
import functools

import numpy as np
import jax
import jax.numpy as jnp
from jax import lax
from jax.experimental import pallas as pl
from jax.experimental.pallas import tpu as pltpu

F32 = jnp.float32
BF16 = jnp.bfloat16

D_MODEL = 1024
GROUP_W = 256
MLA_HEADS = 4
MLA_NOPE = 64
MLA_ROPE = 32
MLA_V = 64
MLA_Q_RANK = 256
MLA_KV_RANK = 128
ROPE_THETA = 10000.0
CONV_K = 3
GMLP_GROUPS = 4
GMLP_CHUNK = 128
NSA_HEADS = 4
NSA_DH = 64
CMP_LEN = 32
CMP_STRIDE = 16
CMP_HIDDEN = 256
SEL_LEN = 64
N_SELECT = 16
WINDOW = 512
FORCE_SCORE = 1e4
D_FF = 3584
N_EXPERTS = 8
EPS = 1e-6
NEG = -1e30
LOG2E = 1.4426950408889634

LANES = 128
VMEM_LIMIT = 56 * 1024 * 1024

_IN_SIZES = (256, 128, 32, 256, 256, 256, 256, 256, 256, 64, 64, 64, 64, 64, 64, 12)
_IN_OFF = np.concatenate([[0], np.cumsum(_IN_SIZES)]).tolist()

_W_MLA, _W_CONV, _W_GMLP, _W_NQ, _W_KV, _W_GATE = 640, 768, 512, 256, 128, 128
_IN_SPLITS = (_W_MLA, _W_CONV, _W_GMLP, _W_NQ, _W_KV, _W_KV, _W_KV, _W_GATE)
_D_IN_P = sum(_IN_SPLITS)

_NT = (((1,), (1,)), ((), ()))


def _cparams(sem, vmem=VMEM_LIMIT):
    return pltpu.CompilerParams(dimension_semantics=sem, vmem_limit_bytes=vmem)


def _rms(x, g):
    return x * lax.rsqrt(jnp.mean(x * x, axis=-1, keepdims=True) + EPS) * g


def _dot(a, b):
    return jnp.dot(a, b, preferred_element_type=F32)


def _dot_exact(a, b):
    return jnp.dot(a, b, preferred_element_type=F32, precision=lax.Precision.HIGHEST)


def _gelu(x):
    return 0.5 * x * (1.0 + jnp.tanh(0.7978845608028654 * (x + 0.044715 * (x * x * x))))


def _in_proj_kernel(h_ref, g_ref, w_ref, *o_refs):
    xn = _rms(h_ref[...], g_ref[...]).astype(BF16)
    r = _dot(xn, w_ref[...]).astype(BF16)
    off = 0
    for o_ref, w in zip(o_refs, _IN_SPLITS):
        o_ref[...] = r[:, off:off + w]
        off += w


def in_proj(h, g, w):
    T, D = h.shape
    tm = min(512, T)
    return pl.pallas_call(
        _in_proj_kernel, grid=(T // tm,),
        in_specs=[pl.BlockSpec((tm, D), lambda i: (i, 0)),
                  pl.BlockSpec((1, D), lambda i: (0, 0)),
                  pl.BlockSpec((D, _D_IN_P), lambda i: (0, 0))],
        out_specs=[pl.BlockSpec((tm, w_), lambda i: (i, 0)) for w_ in _IN_SPLITS],
        out_shape=[jax.ShapeDtypeStruct((T, w_), BF16) for w_ in _IN_SPLITS],
        compiler_params=_cparams(("parallel",)), name="in_proj")(h, g, w)


def prep_w_in(w_in):
    o = _IN_OFF
    L, D, _ = w_in.shape
    z = lambda n: jnp.zeros((L, D, n), F32)
    kr = w_in[:, :, o[2]:o[3]]
    half = MLA_ROPE // 2
    kr_sw = jnp.concatenate([kr[..., half:], kr[..., :half]], axis=-1)
    cols = [w_in[:, :, o[0]:o[2]],
            kr, z(96), kr_sw, z(96),
            w_in[:, :, o[3]:o[6]],
            w_in[:, :, o[6]:o[8]],
            w_in[:, :, o[8]:o[9]] * (NSA_DH ** -0.5 * LOG2E),
            w_in[:, :, o[9]:o[15]],
            w_in[:, :, o[15]:o[16]], z(_W_GATE - 12)]
    return jnp.concatenate(cols, axis=-1).astype(BF16)


_MLA_QW = 2 * LANES


def _rope_tables_kernel(pos_ref, freq_ref, sign_ref, c_ref, s_ref):
    ang = pos_ref[...] * freq_ref[...]
    c_ref[...] = jnp.cos(ang)
    s_ref[...] = jnp.sin(ang) * sign_ref[...]


def rope_tables(pos, freq, sign):
    T = pos.shape[0]
    tm = min(1024, T)
    row = pl.BlockSpec((tm, LANES), lambda i: (i, 0))
    one = pl.BlockSpec((1, LANES), lambda i: (0, 0))
    return pl.pallas_call(
        _rope_tables_kernel, grid=(T // tm,),
        in_specs=[pl.BlockSpec((tm, 1), lambda i: (i, 0)), one, one],
        out_specs=[row, row],
        out_shape=[jax.ShapeDtypeStruct((T, LANES), F32), jax.ShapeDtypeStruct((T, LANES), F32)],
        compiler_params=_cparams(("parallel",)), name="rope_tables")(pos, freq, sign)


def _mla_proj_kernel(x_ref, c_ref, s_ref, qg_ref, kvg_ref, wqn_ref, wukt_ref, wra_ref, wrb_ref, q_ref, k_ref):
    x = x_ref[...].astype(F32)
    cq, ckv = x[:, :256], x[:, 256:384]
    kra, krb = x[:, 384:512], x[:, 512:640]
    c, s = c_ref[...], s_ref[...]
    cqn = _rms(cq, qg_ref[...]).astype(BF16)
    qn = _dot(cqn, wqn_ref[...])
    ra = _dot(cqn, wra_ref[...])
    rb = _dot(cqn, wrb_ref[...])
    for h in range(MLA_HEADS):
        hs = slice(h * LANES, (h + 1) * LANES)
        q_ref[:, h * _MLA_QW:h * _MLA_QW + LANES] = _dot(qn[:, hs].astype(BF16), wukt_ref[h]).astype(BF16)
        q_ref[:, h * _MLA_QW + LANES:(h + 1) * _MLA_QW] = (ra[:, hs] * c + rb[:, hs] * s).astype(BF16)
    k_ref[:, :LANES] = _rms(ckv, kvg_ref[...]).astype(BF16)
    k_ref[:, LANES:] = (kra * c + krb * s).astype(BF16)


def mla_proj(x_mla, cos, sin, qg, kvg, wqn, wukt, wra, wrb):
    T = x_mla.shape[0]
    tm = min(512, T)
    full = lambda a: pl.BlockSpec(a.shape, lambda i: (0,) * a.ndim)
    return pl.pallas_call(
        _mla_proj_kernel, grid=(T // tm,),
        in_specs=[pl.BlockSpec((tm, _W_MLA), lambda i: (i, 0)),
                  pl.BlockSpec((tm, LANES), lambda i: (i, 0)),
                  pl.BlockSpec((tm, LANES), lambda i: (i, 0)),
                  full(qg), full(kvg), full(wqn), full(wukt), full(wra), full(wrb)],
        out_specs=[pl.BlockSpec((tm, MLA_HEADS * _MLA_QW), lambda i: (i, 0)),
                   pl.BlockSpec((tm, _MLA_QW), lambda i: (i, 0))],
        out_shape=[jax.ShapeDtypeStruct((T, MLA_HEADS * _MLA_QW), BF16),
                   jax.ShapeDtypeStruct((T, _MLA_QW), BF16)],
        compiler_params=_cparams(("parallel",)), name="mla_proj")(
            x_mla, cos, sin, qg, kvg, wqn, wukt, wra, wrb)


def prep_mla(w_uq, w_ukv):
    L = w_uq.shape[0]
    scale = (MLA_NOPE + MLA_ROPE) ** -0.5 * LOG2E
    half = MLA_ROPE // 2
    hq = MLA_NOPE + MLA_ROPE
    hk = MLA_NOPE + MLA_V
    zq = lambda n: jnp.zeros((L, MLA_Q_RANK, n), F32)
    qn, ra, rb, ukt, uvp = [], [], [], [], []
    for h in range(MLA_HEADS):
        rp = w_uq[:, :, h * hq + MLA_NOPE:(h + 1) * hq]
        qn += [w_uq[:, :, h * hq:h * hq + MLA_NOPE], zq(LANES - MLA_NOPE)]
        ra += [rp, zq(LANES - MLA_ROPE)]
        rb += [jnp.concatenate([rp[..., half:], rp[..., :half]], axis=-1), zq(LANES - MLA_ROPE)]
        w_uk = w_ukv[:, :, h * hk:h * hk + MLA_NOPE]
        ukt.append(jnp.pad(jnp.swapaxes(w_uk, 1, 2), ((0, 0), (0, LANES - MLA_NOPE), (0, 0))))
        w_uv = w_ukv[:, :, h * hk + MLA_NOPE:(h + 1) * hk]
        uvp.append(jnp.pad(w_uv, ((0, 0), (0, 0), (h * MLA_V, (MLA_HEADS - 1 - h) * MLA_V))))
    wqn = (jnp.concatenate(qn, axis=-1) * scale).astype(BF16)
    wra = (jnp.concatenate(ra, axis=-1) * scale).astype(BF16)
    wrb = (jnp.concatenate(rb, axis=-1) * scale).astype(BF16)
    wukt = jnp.stack(ukt, axis=1).astype(BF16)
    wuvp = jnp.stack(uvp, axis=1).astype(BF16)
    inv = ROPE_THETA ** (-jnp.arange(half, dtype=F32) / half)
    pad = jnp.zeros((LANES - MLA_ROPE,), F32)
    freq = jnp.concatenate([inv, inv, pad])[None, :]
    sign = jnp.concatenate([-jnp.ones((half,), F32), jnp.ones((half,), F32), pad])[None, :]
    return wqn, wukt, wra, wrb, wuvp, freq, sign


def _causal_mask(t):
    row = lax.broadcasted_iota(jnp.int32, (t, t), 0)
    col = lax.broadcasted_iota(jnp.int32, (t, t), 1)
    return row >= col


def _lane_fold(x, op):
    r = x[:, :LANES]
    for c in range(1, x.shape[1] // LANES):
        r = op(r, x[:, c * LANES:(c + 1) * LANES])
    return r


class _Softmax:
    def __init__(self, q, m_ref, l_ref, acc_ref):
        self.q, self.m_ref, self.l_ref, self.acc_ref = q, m_ref, l_ref, acc_ref
        m_ref[...] = jnp.full(m_ref.shape, NEG, F32)
        l_ref[...] = jnp.zeros(l_ref.shape, F32)
        acc_ref[...] = jnp.zeros(acc_ref.shape, F32)

    def tile(self, k, v, mask_fn=None):
        s = lax.dot_general(self.q, k, _NT, preferred_element_type=F32)
        if mask_fn is not None:
            s = mask_fn(s)
        m_old = self.m_ref[...]
        m_new = jnp.maximum(m_old, jnp.max(_lane_fold(s, jnp.maximum), axis=-1, keepdims=True))
        alpha = jnp.exp2(m_old - m_new)
        parts = [jnp.exp2(s[:, c * LANES:(c + 1) * LANES] - m_new) for c in range(s.shape[1] // LANES)]
        lsum = parts[0]
        for p in parts[1:]:
            lsum = lsum + p
        self.l_ref[...] = alpha * self.l_ref[...] + lsum
        p = jnp.concatenate(parts, axis=1).astype(BF16)
        self.acc_ref[...] = alpha * self.acc_ref[...] + _dot(p, v)
        self.m_ref[...] = m_new

    def result(self):
        return self.acc_ref[...] / jnp.sum(self.l_ref[...], axis=-1, keepdims=True)


def _local_rc(nh, tq):
    row = lax.broadcasted_iota(jnp.int32, (nh * tq, tq), 0) & (tq - 1)
    col = lax.broadcasted_iota(jnp.int32, (nh * tq, tq), 1)
    return row, col


def _mask_last_block(s, keep, tq):
    w = s.shape[1]
    last = jnp.where(keep, s[:, w - tq:], NEG)
    return last if w == tq else jnp.concatenate([s[:, :w - tq], last], axis=1)


def _causal_sweep(sm, i, nh, tq, tk, load):
    n_full = (i * tq) // tk

    def body(j, c):
        sm.tile(*load(pl.multiple_of(j * tk, tk), tk))
        return c

    lax.fori_loop(0, n_full, body, 0)
    row, col = _local_rc(nh, tq)
    per = tk // tq
    for r in range(per):
        @pl.when(i % per == r)
        def _():
            k, v = load(pl.multiple_of(n_full * tk, tk), (r + 1) * tq)
            sm.tile(k, v, lambda s: _mask_last_block(s, col <= row, tq))


def _mla_attn_kernel(q_ref, k_ref, wuvp_ref, o_ref, qs_ref, m_ref, l_ref, acc_ref, *, tq, tk):
    i = pl.program_id(1)
    for h in range(MLA_HEADS):
        qs_ref[h * tq:(h + 1) * tq, :] = q_ref[:, h * _MLA_QW:(h + 1) * _MLA_QW]
    sm = _Softmax(qs_ref[...], m_ref, l_ref, acc_ref)

    def load(start, width):
        k = k_ref[pl.ds(start, width), :]
        return k, k[:, :LANES]

    _causal_sweep(sm, i, MLA_HEADS, tq, tk, load)
    o_lat = sm.result().astype(BF16)
    out = _dot(o_lat[:tq], wuvp_ref[0])
    for h in range(1, MLA_HEADS):
        out = out + _dot(o_lat[h * tq:(h + 1) * tq], wuvp_ref[h])
    o_ref[...] = out.astype(BF16)


_ATT_TQ, _ATT_TK = 256, 1024


def mla_attn(q, k, wuvp, B, S):
    tq, tk = min(_ATT_TQ, S), min(_ATT_TK, S)
    assert tq & (tq - 1) == 0
    M = MLA_HEADS * tq
    q3, k3 = q.reshape(B, S, MLA_HEADS * _MLA_QW), k.reshape(B, S, _MLA_QW)
    out = pl.pallas_call(
        functools.partial(_mla_attn_kernel, tq=tq, tk=tk), grid=(B, S // tq),
        in_specs=[pl.BlockSpec((None, tq, MLA_HEADS * _MLA_QW), lambda b, i: (b, i, 0)),
                  pl.BlockSpec((None, S, _MLA_QW), lambda b, i: (b, 0, 0)),
                  pl.BlockSpec(wuvp.shape, lambda b, i: (0, 0, 0))],
        out_specs=pl.BlockSpec((None, tq, 256), lambda b, i: (b, i, 0)),
        out_shape=jax.ShapeDtypeStruct((B, S, 256), BF16),
        scratch_shapes=[pltpu.VMEM((M, _MLA_QW), BF16), pltpu.VMEM((M, LANES), F32),
                        pltpu.VMEM((M, LANES), F32), pltpu.VMEM((M, LANES), F32)],
        compiler_params=_cparams(("parallel", "arbitrary")), name="mla_attn")(q3, k3, wuvp)
    return out.reshape(B * S, 256)


def _conv_gmlp_kernel(xc_ref, halo_ref, xg_ref, cw_ref, ln_ref, ws_ref, bs_ref, o_ref, *, tiles_per_seq):
    i = pl.program_id(0)
    tm = xc_ref.shape[0]
    xc = xc_ref[...].astype(F32)
    cb, u = xc[:, :256], xc[:, 256:512] * xc[:, 512:768]
    hx = halo_ref[...].astype(F32)
    keep = jnp.where(i % tiles_per_seq == 0, 0.0, 1.0)
    hu = hx[:, 256:512] * hx[:, 512:768] * keep
    p1, p2 = hu[7:8, :], hu[6:7, :]
    row = lax.broadcasted_iota(jnp.int32, (tm, 256), 0)
    u1 = jnp.where(row == 0, p1, pltpu.roll(u, 1, axis=0))
    u2 = jnp.where(row == 0, p2, jnp.where(row == 1, p1, pltpu.roll(u, 2, axis=0)))
    cw = cw_ref[...]
    y_conv = cb * (cw[0:1, :] * u2 + cw[1:2, :] * u1 + cw[2:3, :] * u)
    o_ref[:, :256] = y_conv.astype(BF16)

    xg = xg_ref[...].astype(F32)
    gu = _gelu(xg[:, :256])
    gv = _gelu(xg[:, 256:512])
    mu = jnp.mean(gv, axis=-1, keepdims=True)
    var = jnp.mean(jnp.square(gv - mu), axis=-1, keepdims=True)
    vn = (gv - mu) * lax.rsqrt(var + EPS) * ln_ref[...]
    lane_g = lax.broadcasted_iota(jnp.int32, (GMLP_CHUNK, 256), 1) // (256 // GMLP_GROUPS)
    tril = _causal_mask(GMLP_CHUNK)
    for c in range(tm // GMLP_CHUNK):
        vc = vn[c * GMLP_CHUNK:(c + 1) * GMLP_CHUNK, :]
        sv = bs_ref[...]
        for g in range(GMLP_GROUPS):
            wg = jnp.where(tril, ws_ref[g], 0.0).astype(BF16)
            sv = sv + _dot(wg, jnp.where(lane_g == g, vc, 0.0).astype(BF16))
        o_ref[c * GMLP_CHUNK:(c + 1) * GMLP_CHUNK, 256:512] = (
            gu[c * GMLP_CHUNK:(c + 1) * GMLP_CHUNK, :] * sv).astype(BF16)


def conv_gmlp(x_conv, x_gmlp, conv_w8, ln_g, w_s, b_exp, S):
    T = x_conv.shape[0]
    tm = min(512, S)
    full = lambda a: pl.BlockSpec(a.shape, lambda i: (0,) * a.ndim)
    return pl.pallas_call(
        functools.partial(_conv_gmlp_kernel, tiles_per_seq=S // tm), grid=(T // tm,),
        in_specs=[pl.BlockSpec((tm, _W_CONV), lambda i: (i, 0)),
                  pl.BlockSpec((8, _W_CONV), lambda i: (jnp.maximum(i * (tm // 8) - 1, 0), 0)),
                  pl.BlockSpec((tm, _W_GMLP), lambda i: (i, 0)),
                  full(conv_w8), full(ln_g), full(w_s), full(b_exp)],
        out_specs=pl.BlockSpec((tm, 512), lambda i: (i, 0)),
        out_shape=jax.ShapeDtypeStruct((T, 512), BF16),
        compiler_params=_cparams(("parallel",)), name="conv_gmlp")(
            x_conv, x_conv, x_gmlp, conv_w8, ln_g, w_s, b_exp)


def _compress_kernel(z_ref, pea_ref, peb_ref, w1a_ref, w1b_ref, w2_ref, o_ref):
    z = z_ref[...]
    n = z.shape[0]
    a = _dot(z, w1a_ref[...])
    b = _dot(z, w1b_ref[...])
    c = _dot_exact(pea_ref[...], w1a_ref[...].astype(F32)) + _dot_exact(peb_ref[...], w1b_ref[...].astype(F32))
    pre = a + pltpu.roll(b, n - 1, axis=0) + c[0:1, :]
    o_ref[...] = _dot(_gelu(pre).astype(BF16), w2_ref[...]).astype(BF16)


def compress(z2, pea, peb, w1a, w1b, w2):
    B, n, K = z2.shape
    full = lambda a: pl.BlockSpec(a.shape, lambda b: (0,) * a.ndim)
    return pl.pallas_call(
        _compress_kernel, grid=(B,),
        in_specs=[pl.BlockSpec((None, n, K), lambda b: (b, 0, 0)),
                  full(pea), full(peb), full(w1a), full(w1b), full(w2)],
        out_specs=pl.BlockSpec((None, n, LANES), lambda b: (b, 0, 0)),
        out_shape=jax.ShapeDtypeStruct((B, n, LANES), BF16),
        compiler_params=_cparams(("parallel",)), name="nsa_compress")(z2, pea, peb, w1a, w1b, w2)


def prep_compress(pe_k, w1_k, w2_k, pe_v, w1_v, w2_v):
    half = CMP_LEN // 2

    def w1_half(w1k, w1v):
        wk = w1k.reshape(half, NSA_DH, CMP_HIDDEN)
        wv = w1v.reshape(half, NSA_DH, CMP_HIDDEN)
        zk = jnp.zeros_like(wk)
        top = jnp.concatenate([wk, zk], axis=-1)
        bot = jnp.concatenate([zk, wv], axis=-1)
        return jnp.concatenate([top, bot], axis=1).reshape(half * 2 * NSA_DH, 2 * CMP_HIDDEN)

    n1 = half * NSA_DH
    w1a = w1_half(w1_k[:n1], w1_v[:n1]).astype(BF16)
    w1b = w1_half(w1_k[n1:], w1_v[n1:]).astype(BF16)

    def pe_half(pk, pv):
        row = jnp.concatenate([pk, pv], axis=-1).reshape(1, half * 2 * NSA_DH)
        return jnp.broadcast_to(row, (8, half * 2 * NSA_DH))

    pea = pe_half(pe_k[:half], pe_v[:half])
    peb = pe_half(pe_k[half:], pe_v[half:])
    z = jnp.zeros((CMP_HIDDEN, NSA_DH), F32)
    w2 = jnp.concatenate([jnp.concatenate([w2_k, z], axis=1),
                          jnp.concatenate([z, w2_v], axis=1)], axis=0).astype(BF16)
    return pea, peb, w1a, w1b, w2


def _head_select(nh):
    sel = np.zeros((nh, nh * NSA_DH, LANES), np.float32)
    place = np.zeros((nh, LANES, nh * NSA_DH), np.float32)
    for h in range(nh):
        for d in range(NSA_DH):
            sel[h, h * NSA_DH + d, d] = 1.0
            place[h, NSA_DH + d, h * NSA_DH + d] = 1.0
    return jnp.asarray(sel, BF16), jnp.asarray(place, BF16)


def _stack_heads(q, sel_ref, qs_ref):
    for h in range(NSA_HEADS):
        qs_ref[h] = _dot(q, sel_ref[h]).astype(BF16)


def _place_heads(o_list, place_ref):
    out = _dot(o_list[0].astype(BF16), place_ref[0])
    for h in range(1, NSA_HEADS):
        out = out + _dot(o_list[h].astype(BF16), place_ref[h])
    return out


def _gate(gl, h, branch):
    c = h * 3 + branch
    return 1.0 / (1.0 + jnp.exp(-gl[:, c:c + 1]))


def _nsa_cmp_kernel(q_ref, kv_ref, gl_ref, msel_ref, sel_ref, place_ref, o_ref, bias_ref, qs_ref, *, t, ns):
    i = pl.program_id(1)
    nc = kv_ref.shape[0]
    _stack_heads(q_ref[...], sel_ref, qs_ref)
    kv = kv_ref[...]
    gl = gl_ref[...].astype(F32)
    tpos = i * t + lax.broadcasted_iota(jnp.int32, (t, nc), 0)
    blk_end = lax.broadcasted_iota(jnp.int32, (t, nc), 1) * CMP_STRIDE + (CMP_LEN - 1)
    cmask = blk_end <= tpos
    p_sum = jnp.zeros((t, nc), F32)
    outs = []
    for h in range(NSA_HEADS):
        s = lax.dot_general(qs_ref[h], kv, _NT, preferred_element_type=F32)
        s = jnp.where(cmask, s, NEG)
        m = jnp.max(s, axis=-1, keepdims=True)
        e = jnp.where(cmask, jnp.exp2(s - m), 0.0)
        l = jnp.sum(e, axis=-1, keepdims=True)
        p = e / jnp.where(l > 0.0, l, 1.0)
        p_sum = p_sum + p
        outs.append(_gate(gl, h, 0) * _dot(p.astype(BF16), kv))
    o_ref[...] = _place_heads(outs, place_ref).astype(BF16)

    p_sel = _dot_exact(p_sum, msel_ref[...])
    blk = lax.broadcasted_iota(jnp.int32, (t, LANES), 1) - NSA_DH
    blk_f = blk.astype(F32)
    cur = (i * t + lax.broadcasted_iota(jnp.int32, (t, LANES), 0)) // SEL_LEN
    forced = (blk == 0) | (blk == cur) | (blk == cur - 1)
    v = jnp.where(forced, FORCE_SCORE, jnp.where(blk <= cur, p_sel, -1.0))
    v = jnp.where((blk >= 0) & (blk < ns), v, -jnp.inf)
    chosen = blk < 0
    for _ in range(min(N_SELECT, ns)):
        mx = jnp.max(v, axis=-1, keepdims=True)
        idx = jnp.min(jnp.where(v == mx, blk_f, float(LANES)), axis=-1, keepdims=True)
        hit = blk_f == idx
        chosen = chosen | hit
        v = jnp.where(hit, -jnp.inf, v)
    bias_ref[...] = jnp.where(chosen, 0.0, NEG).astype(BF16)


def nsa_cmp(q, kcv, gl, msel, sel, place, B, S):
    t = min(1024, S)
    nc = kcv.shape[1]
    ns = S // SEL_LEN
    full = lambda a: pl.BlockSpec(a.shape, lambda b, i: (0,) * a.ndim)
    return pl.pallas_call(
        functools.partial(_nsa_cmp_kernel, t=t, ns=ns), grid=(B, S // t),
        in_specs=[pl.BlockSpec((None, t, 256), lambda b, i: (b, i, 0)),
                  pl.BlockSpec((None, nc, LANES), lambda b, i: (b, 0, 0)),
                  pl.BlockSpec((None, t, LANES), lambda b, i: (b, i, 0)),
                  full(msel), full(sel), full(place)],
        out_specs=[pl.BlockSpec((None, t, 256), lambda b, i: (b, i, 0)),
                   pl.BlockSpec((None, t, LANES), lambda b, i: (b, i, 0))],
        out_shape=[jax.ShapeDtypeStruct((B, S, 256), BF16), jax.ShapeDtypeStruct((B, S, LANES), BF16)],
        scratch_shapes=[pltpu.VMEM((NSA_HEADS, t, LANES), BF16)],
        compiler_params=_cparams(("parallel", "arbitrary")), name="nsa_cmp")(
            q.reshape(B, S, 256), kcv, gl.reshape(B, S, LANES), msel, sel, place)


def cmp_to_sel(S):
    nc = S // CMP_STRIDE - CMP_LEN // CMP_STRIDE + 1
    ns = S // SEL_LEN
    cs = np.arange(nc)[:, None] * CMP_STRIDE
    ss = np.arange(ns)[None, :] * SEL_LEN
    ov = np.clip(np.minimum(cs + CMP_LEN, ss + SEL_LEN) - np.maximum(cs, ss), 0, None)
    assert ns <= LANES - NSA_DH, "selection blocks must fit the lanes beside one head's query"
    m = np.zeros((S // CMP_STRIDE, LANES), np.float32)
    m[:nc, NSA_DH:NSA_DH + ns] = ov.astype(np.float32) / np.float32(CMP_LEN)
    return jnp.asarray(m)


def _nsa_attn_kernel(q_ref, ksel_ref, kwin_ref, bias_ref, ocmp_ref, gl_ref, sel_ref, place_ref,
                     o_ref, kx_ref, kwp_ref, qs_ref, qw_ref, m_ref, l_ref, acc_ref, *, tq, tk):
    i = pl.program_id(1)
    S = ksel_ref.shape[0]

    @pl.when(i == 0)
    def _():
        r = lax.broadcasted_iota(jnp.int32, (S, LANES), 0)
        c = lax.broadcasted_iota(jnp.int32, (S, LANES), 1)
        onehot = jnp.where(r // SEL_LEN == c - NSA_DH, 1.0, 0.0)
        kx_ref[...] = jnp.where(c < NSA_DH, ksel_ref[...].astype(F32), onehot).astype(BF16)
        kwp_ref[:WINDOW, :] = jnp.zeros((WINDOW, LANES), BF16)
        kwp_ref[WINDOW:, :] = kwin_ref[...]

    q = q_ref[...]
    bias = bias_ref[...].astype(F32)
    for h in range(NSA_HEADS):
        qs = _dot(q, sel_ref[h])
        qw_ref[h * tq:(h + 1) * tq, :] = qs.astype(BF16)
        qs_ref[h * tq:(h + 1) * tq, :] = (qs + bias).astype(BF16)

    def heads(x):
        return [x[h * tq:(h + 1) * tq] for h in range(NSA_HEADS)]

    sm = _Softmax(qs_ref[...], m_ref, l_ref, acc_ref)
    _causal_sweep(sm, i, NSA_HEADS, tq, tk,
                  lambda start, width: (kx_ref[pl.ds(start, width), :], ksel_ref[pl.ds(start, width), :]))
    o_sel = heads(sm.result())

    sm = _Softmax(qw_ref[...], m_ref, l_ref, acc_ref)
    row, col = _local_rc(NSA_HEADS, tq)
    nblk = WINDOW // tq + 1

    def win_mask(s):
        blocks = []
        for c in range(nblk):
            sc = s[:, c * tq:(c + 1) * tq]
            if c == 0:
                sc = jnp.where(row < col, sc, NEG)
            if c == nblk - 1:
                sc = jnp.where(col <= row, sc, NEG)
            else:
                sc = sc + jnp.where(i >= nblk - 1 - c, 0.0, NEG)
            blocks.append(sc)
        return jnp.concatenate(blocks, axis=1)

    kw = kwp_ref[pl.ds(pl.multiple_of(i * tq, tq), WINDOW + tq), :]
    sm.tile(kw, kw, win_mask)
    o_win = heads(sm.result())

    gl = gl_ref[...].astype(F32)
    gated = [_gate(gl, h, 1) * o_sel[h] + _gate(gl, h, 2) * o_win[h] for h in range(NSA_HEADS)]
    o_ref[...] = (ocmp_ref[...].astype(F32) + _place_heads(gated, place_ref)).astype(BF16)


def nsa_attn(q, ksel, kwin, bias, o_cmp, gl, sel, place, B, S):
    tq, tk = min(_ATT_TQ, S), min(_ATT_TK, S)
    assert WINDOW % tq == 0
    M = NSA_HEADS * tq
    full = lambda a: pl.BlockSpec(a.shape, lambda b, i: (0,) * a.ndim)
    tile2 = lambda w: pl.BlockSpec((None, tq, w), lambda b, i: (b, i, 0))
    seq = pl.BlockSpec((None, S, LANES), lambda b, i: (b, 0, 0))
    out = pl.pallas_call(
        functools.partial(_nsa_attn_kernel, tq=tq, tk=tk), grid=(B, S // tq),
        in_specs=[tile2(256), seq, seq, tile2(LANES), tile2(256), tile2(LANES), full(sel), full(place)],
        out_specs=tile2(256),
        out_shape=jax.ShapeDtypeStruct((B, S, 256), BF16),
        scratch_shapes=[pltpu.VMEM((S, LANES), BF16), pltpu.VMEM((S + WINDOW, LANES), BF16),
                        pltpu.VMEM((M, LANES), BF16), pltpu.VMEM((M, LANES), BF16),
                        pltpu.VMEM((M, LANES), F32), pltpu.VMEM((M, LANES), F32), pltpu.VMEM((M, LANES), F32)],
        compiler_params=_cparams(("parallel", "arbitrary")), name="nsa_attn")(
            q.reshape(B, S, 256), ksel.reshape(B, S, LANES), kwin.reshape(B, S, LANES), bias, o_cmp,
            gl.reshape(B, S, LANES), sel, place)
    return out.reshape(B * S, 256)


def _mix_out_kernel(ymla_ref, ycg_ref, ynsa_ref, h_ref, gn_ref, wo_ref, o_ref):
    gn = gn_ref[...]
    parts = [ymla_ref[...].astype(F32), ycg_ref[:, :256].astype(F32), ycg_ref[:, 256:].astype(F32),
             ynsa_ref[...].astype(F32)]
    yn = [_rms(p, gn[:, k * GROUP_W:(k + 1) * GROUP_W]).astype(BF16) for k, p in enumerate(parts)]
    o_ref[...] = h_ref[...] + _dot(jnp.concatenate(yn, axis=1), wo_ref[...])


def mix_out(y_mla, y_cg, y_nsa, h, gn, wo):
    T, D = h.shape
    tm = min(512, T)
    row = lambda w: pl.BlockSpec((tm, w), lambda i: (i, 0))
    return pl.pallas_call(
        _mix_out_kernel, grid=(T // tm,),
        in_specs=[row(256), row(512), row(256), row(D),
                  pl.BlockSpec((1, D), lambda i: (0, 0)), pl.BlockSpec((D, D), lambda i: (0, 0))],
        out_specs=row(D), out_shape=jax.ShapeDtypeStruct((T, D), F32),
        compiler_params=_cparams(("parallel",)), name="mix_out")(y_mla, y_cg, y_nsa, h, gn, wo)


def _ffn_step(j, nj, x_ref, g_ref, wg_ref, wu_ref, wd_ref, o_ref, hn_ref, acc_ref, residual):
    @pl.when(j == 0)
    def _():
        hn_ref[...] = _rms(x_ref[...], g_ref[...]).astype(BF16)
        acc_ref[...] = jnp.zeros(acc_ref.shape, F32)

    hn = hn_ref[...]
    a = _dot(hn, wg_ref[...])
    b = _dot(hn, wu_ref[...])
    act = (a / (1.0 + jnp.exp(-a)) * b).astype(BF16)
    acc_ref[...] += _dot(act, wd_ref[...])

    @pl.when(j == nj - 1)
    def _():
        o_ref[...] = (x_ref[...] + acc_ref[...]) if residual else acc_ref[...]


def _ffn_kernel(x_ref, g_ref, wg_ref, wu_ref, wd_ref, o_ref, hn_ref, acc_ref):
    _ffn_step(pl.program_id(1), pl.num_programs(1), x_ref, g_ref, wg_ref, wu_ref, wd_ref, o_ref, hn_ref,
              acc_ref, True)


_FFN_TF = 512


def ffn_dense(h, g, wg, wu, wd):
    T, D = h.shape
    F = wg.shape[1]
    tm = min(1024, T)
    return pl.pallas_call(
        _ffn_kernel, grid=(T // tm, F // _FFN_TF),
        in_specs=[pl.BlockSpec((tm, D), lambda i, j: (i, 0)),
                  pl.BlockSpec((1, D), lambda i, j: (0, 0)),
                  pl.BlockSpec((D, _FFN_TF), lambda i, j: (0, j)),
                  pl.BlockSpec((D, _FFN_TF), lambda i, j: (0, j)),
                  pl.BlockSpec((_FFN_TF, D), lambda i, j: (j, 0))],
        out_specs=pl.BlockSpec((tm, D), lambda i, j: (i, 0)),
        out_shape=jax.ShapeDtypeStruct((T, D), F32),
        scratch_shapes=[pltpu.VMEM((tm, D), BF16), pltpu.VMEM((tm, D), F32)],
        compiler_params=_cparams(("parallel", "arbitrary")), name="ffn_dense")(h, g, wg, wu, wd)


def _ffn_grouped_kernel(te_ref, nu_ref, x_ref, g_ref, wg_ref, wu_ref, wd_ref, o_ref, hn_ref, acc_ref):
    del te_ref
    i, j, nj = pl.program_id(0), pl.program_id(1), pl.num_programs(1)
    used = i < nu_ref[0]

    @pl.when(used)
    def _():
        _ffn_step(j, nj, x_ref, g_ref, wg_ref, wu_ref, wd_ref, o_ref, hn_ref, acc_ref, False)

    @pl.when(jnp.logical_not(used))
    def _():
        o_ref[...] = jnp.zeros(o_ref.shape, F32)


def ffn_grouped(xs, g, wg, wu, wd, tile_expert, n_used, tm):
    NP, D = xs.shape
    F = wg.shape[2]
    nf = F // _FFN_TF

    def jj(i, j, nu):
        return jnp.where(i < nu[0], j, nf - 1)

    grid_spec = pltpu.PrefetchScalarGridSpec(
        num_scalar_prefetch=2, grid=(NP // tm, nf),
        in_specs=[pl.BlockSpec((tm, D), lambda i, j, te, nu: (i, 0)),
                  pl.BlockSpec((1, D), lambda i, j, te, nu: (0, 0)),
                  pl.BlockSpec((None, D, _FFN_TF), lambda i, j, te, nu: (te[i], 0, jj(i, j, nu))),
                  pl.BlockSpec((None, D, _FFN_TF), lambda i, j, te, nu: (te[i], 0, jj(i, j, nu))),
                  pl.BlockSpec((None, _FFN_TF, D), lambda i, j, te, nu: (te[i], jj(i, j, nu), 0))],
        out_specs=pl.BlockSpec((tm, D), lambda i, j, te, nu: (i, 0)),
        scratch_shapes=[pltpu.VMEM((tm, D), BF16), pltpu.VMEM((tm, D), F32)])
    return pl.pallas_call(
        _ffn_grouped_kernel, grid_spec=grid_spec,
        out_shape=jax.ShapeDtypeStruct((NP, D), F32),
        compiler_params=_cparams(("arbitrary", "arbitrary")), name="ffn_grouped")(
            tile_expert, n_used, xs, g, wg, wu, wd)


def _router_kernel(h_ref, g_ref, wr_ref, o_ref):
    hn = _rms(h_ref[...], g_ref[...])
    logits = _dot_exact(hn, wr_ref[...])
    tm = logits.shape[0]
    lane = lax.broadcasted_iota(jnp.int32, (tm, LANES), 1)
    lane_f = lane.astype(F32)
    lg = jnp.where(lane < N_EXPERTS, logits, -jnp.inf)
    m1 = jnp.max(lg, axis=-1, keepdims=True)
    i1 = jnp.min(jnp.where(lg == m1, lane_f, float(LANES)), axis=-1, keepdims=True)
    lg2 = jnp.where(lane_f == i1, -jnp.inf, lg)
    m2 = jnp.max(lg2, axis=-1, keepdims=True)
    i2 = jnp.min(jnp.where(lg2 == m2, lane_f, float(LANES)), axis=-1, keepdims=True)
    e = jnp.exp(m2 - m1)
    w1 = 1.0 / (1.0 + e)
    w2 = e / (1.0 + e)
    o_ref[...] = jnp.where(lane == 0, i1,
                           jnp.where(lane == 1, i2,
                                     jnp.where(lane == 2, w1, jnp.where(lane == 3, w2, 0.0))))


def router(h, g, wr):
    T, D = h.shape
    tm = min(512, T)
    return pl.pallas_call(
        _router_kernel, grid=(T // tm,),
        in_specs=[pl.BlockSpec((tm, D), lambda i: (i, 0)), pl.BlockSpec((1, D), lambda i: (0, 0)),
                  pl.BlockSpec((D, LANES), lambda i: (0, 0))],
        out_specs=pl.BlockSpec((tm, LANES), lambda i: (i, 0)),
        out_shape=jax.ShapeDtypeStruct((T, LANES), F32),
        compiler_params=_cparams(("parallel",)), name="moe_router")(h, g, wr)


def _row_copy(src, s, dst, d, sem):
    return pltpu.make_async_copy(src.at[pl.ds(s, 1)], dst.at[pl.ds(d, 1)], sem)


def _dispatch_kernel(dest_ref, h_ref, xs_in_ref, xs_ref, sem):
    del xs_in_ref
    tm = h_ref.shape[0]

    def issue(r, c):
        _row_copy(h_ref, r, xs_ref, dest_ref[0, 0, 2 * r], sem).start()
        _row_copy(h_ref, r, xs_ref, dest_ref[0, 0, 2 * r + 1], sem).start()
        return c

    lax.fori_loop(0, tm, issue, 0, unroll=8)
    for _ in range(2):
        pltpu.make_async_copy(h_ref, xs_ref.at[pl.ds(0, tm)], sem).wait()


def dispatch(h, dest, xs_zero):
    T, D = h.shape
    tm = min(512, T)
    dest3 = dest.reshape(T // tm, 1, 2 * tm)
    return pl.pallas_call(
        _dispatch_kernel, grid=(T // tm,),
        in_specs=[pl.BlockSpec((1, 1, 2 * tm), lambda i: (i, 0, 0), memory_space=pltpu.SMEM),
                  pl.BlockSpec((tm, D), lambda i: (i, 0)),
                  pl.BlockSpec(memory_space=pl.ANY)],
        out_specs=pl.BlockSpec(memory_space=pl.ANY),
        out_shape=jax.ShapeDtypeStruct(xs_zero.shape, F32),
        scratch_shapes=[pltpu.SemaphoreType.DMA(())],
        input_output_aliases={2: 0},
        compiler_params=_cparams(("arbitrary",)), name="moe_dispatch")(dest3, h, xs_zero)


def _combine_kernel(dest_ref, dnext_ref, h_ref, route_ref, ys_ref, o_ref, b0_ref, b1_ref, sem):
    i, n = pl.program_id(0), pl.num_programs(0)
    tm = h_ref.shape[0]
    slot = i % 2

    def gather(d_ref, s):
        def issue(r, c):
            _row_copy(ys_ref, d_ref[0, 0, 2 * r], b0_ref.at[s], r, sem.at[s]).start()
            _row_copy(ys_ref, d_ref[0, 0, 2 * r + 1], b1_ref.at[s], r, sem.at[s]).start()
            return c

        lax.fori_loop(0, tm, issue, 0, unroll=8)

    @pl.when(i == 0)
    def _():
        gather(dest_ref, 0)

    @pl.when(i + 1 < n)
    def _():
        gather(dnext_ref, 1 - slot)

    pltpu.make_async_copy(ys_ref.at[pl.ds(0, tm)], b0_ref.at[slot], sem.at[slot]).wait()
    pltpu.make_async_copy(ys_ref.at[pl.ds(0, tm)], b1_ref.at[slot], sem.at[slot]).wait()
    route = route_ref[...]
    o_ref[...] = h_ref[...] + route[:, 2:3] * b0_ref[slot] + route[:, 3:4] * b1_ref[slot]


def combine(h, route, dest, ys):
    T, D = h.shape
    tm = min(256, T)
    n = T // tm
    dest3 = dest.reshape(n, 1, 2 * tm)
    smem = lambda f: pl.BlockSpec((1, 1, 2 * tm), f, memory_space=pltpu.SMEM)
    return pl.pallas_call(
        _combine_kernel, grid=(n,),
        in_specs=[smem(lambda i: (i, 0, 0)), smem(lambda i: (jnp.minimum(i + 1, n - 1), 0, 0)),
                  pl.BlockSpec((tm, D), lambda i: (i, 0)),
                  pl.BlockSpec((tm, LANES), lambda i: (i, 0)),
                  pl.BlockSpec(memory_space=pl.ANY)],
        out_specs=pl.BlockSpec((tm, D), lambda i: (i, 0)),
        out_shape=jax.ShapeDtypeStruct((T, D), F32),
        scratch_shapes=[pltpu.VMEM((2, tm, D), F32), pltpu.VMEM((2, tm, D), F32),
                        pltpu.SemaphoreType.DMA((2,))],
        compiler_params=_cparams(("arbitrary",)), name="moe_combine")(dest3, dest3, h, route, ys)


def moe_ffn(h, g, wr, wg, wu, wd):
    T, D = h.shape
    tm = min(1024, T)
    route = router(h, g, wr)
    e_flat = route[:, :2].astype(jnp.int32).reshape(2 * T)
    onehot = (e_flat[:, None] == jnp.arange(N_EXPERTS, dtype=jnp.int32)[None, :]).astype(jnp.int32)
    csum = jnp.cumsum(onehot, axis=0)
    counts = csum[-1]
    rank = jnp.sum((csum - 1) * onehot, axis=1)
    tiles = (counts + tm - 1) // tm
    tile_end = jnp.cumsum(tiles)
    offs = (tile_end - tiles) * tm
    dest = (jnp.sum(onehot * offs[None, :], axis=1) + rank).astype(jnp.int32)
    n_tiles = (2 * T) // tm + N_EXPERTS
    n_used = tile_end[-1].astype(jnp.int32)
    tile_ids = jnp.minimum(jnp.arange(n_tiles, dtype=jnp.int32), n_used - 1)
    tile_expert = jnp.sum((tile_ids[:, None] >= tile_end[None, :]).astype(jnp.int32), axis=1).astype(jnp.int32)
    xs = dispatch(h, dest, jnp.zeros((n_tiles * tm, D), F32))
    ys = ffn_grouped(xs, g, wg, wu, wd, tile_expert, n_used.reshape(1), tm)
    return combine(h, route, dest, ys)


def _final_norm_kernel(h_ref, g_ref, o_ref):
    o_ref[...] = _rms(h_ref[...], g_ref[...])


def final_rms(h, g):
    T, D = h.shape
    tm = min(1024, T)
    return pl.pallas_call(
        _final_norm_kernel, grid=(T // tm,),
        in_specs=[pl.BlockSpec((tm, D), lambda i: (i, 0)), pl.BlockSpec((1, D), lambda i: (0, 0))],
        out_specs=pl.BlockSpec((tm, D), lambda i: (i, 0)),
        out_shape=jax.ShapeDtypeStruct((T, D), F32),
        compiler_params=_cparams(("parallel",)), name="final_norm")(h, g)


def kernel(x, positions, attn_norm, w_in, mla_q_norm, mla_w_uq, mla_kv_norm, mla_w_ukv, conv_w, gmlp_ln, gmlp_w_s, gmlp_b_s, nsa_pe_k, nsa_w1_k, nsa_w2_k, nsa_pe_v, nsa_w1_v, nsa_w2_v, group_norm, w_o, ffn_norm, dense_w_gate, dense_w_up, dense_w_down, moe_router, moe_w_gate, moe_w_up, moe_w_down, final_norm):
    B, S, D = x.shape
    T = B * S
    depth = w_in.shape[0]
    h = x.reshape(T, D)
    pos = positions.astype(F32).reshape(T, 1)

    w_in_p = prep_w_in(w_in)
    wqn, wukt, wra, wrb, wuvp, freq, sign = prep_mla(mla_w_uq, mla_w_ukv)
    rope_cos, rope_sin = rope_tables(pos, freq, sign)
    conv_w8 = jnp.pad(conv_w, ((0, 0), (0, 8 - CONV_K), (0, 0)))
    b_exp = jnp.repeat(jnp.swapaxes(gmlp_b_s, 1, 2), GROUP_W // GMLP_GROUPS, axis=2)
    sel, place = _head_select(NSA_HEADS)
    msel = cmp_to_sel(S)
    w_o_b = w_o.astype(BF16)
    wr_p = jnp.pad(moe_router, ((0, 0), (0, 0), (0, LANES - N_EXPERTS)))

    for l in range(depth):
        x_mla, x_conv, x_gmlp, x_nq, x_cmp, x_sel, x_win, x_gate = in_proj(h, attn_norm[l][None, :], w_in_p[l])

        q, k = mla_proj(x_mla, rope_cos, rope_sin, mla_q_norm[l][None, :], mla_kv_norm[l][None, :],
                        wqn[l], wukt[l], wra[l], wrb[l])
        y_mla = mla_attn(q, k, wuvp[l], B, S)

        y_cg = conv_gmlp(x_conv, x_gmlp, conv_w8[l], gmlp_ln[l][None, :], gmlp_w_s[l], b_exp[l], S)

        pea, peb, w1a, w1b, w2 = prep_compress(nsa_pe_k[l], nsa_w1_k[l], nsa_w2_k[l],
                                               nsa_pe_v[l], nsa_w1_v[l], nsa_w2_v[l])
        z2 = x_cmp.reshape(B, S // CMP_STRIDE, CMP_STRIDE * LANES)
        kcv = compress(z2, pea, peb, w1a, w1b, w2)
        o_cmp, sel_bias = nsa_cmp(x_nq, kcv, x_gate, msel, sel, place, B, S)
        y_nsa = nsa_attn(x_nq, x_sel, x_win, sel_bias, o_cmp, x_gate, sel, place, B, S)

        h = mix_out(y_mla, y_cg, y_nsa, h, group_norm[l].reshape(1, D), w_o_b[l])

        fg = ffn_norm[l][None, :]
        if l % 2 == 0:
            h = ffn_dense(h, fg, dense_w_gate[l // 2].astype(BF16), dense_w_up[l // 2].astype(BF16),
                          dense_w_down[l // 2].astype(BF16))
        else:
            h = moe_ffn(h, fg, wr_p[l // 2], moe_w_gate[l // 2].astype(BF16), moe_w_up[l // 2].astype(BF16),
                        moe_w_down[l // 2].astype(BF16))
    return final_rms(h, final_norm[None, :]).reshape(B, S, D)
```

```python
import functools

import numpy as np
import jax
import jax.numpy as jnp
from jax import lax
from jax.experimental import pallas as pl
from jax.experimental.pallas import tpu as pltpu

F32 = jnp.float32
BF16 = jnp.bfloat16

D_MODEL = 1024
GROUP_W = 256
MLA_HEADS = 4
MLA_NOPE = 64
MLA_ROPE = 32
MLA_V = 64
MLA_Q_RANK = 256
MLA_KV_RANK = 128
ROPE_THETA = 10000.0
CONV_K = 3
GMLP_GROUPS = 4
GMLP_CHUNK = 128
NSA_HEADS = 4
NSA_DH = 64
CMP_LEN = 32
CMP_STRIDE = 16
CMP_HIDDEN = 256
SEL_LEN = 64
N_SELECT = 16
WINDOW = 512
FORCE_SCORE = 1e4
D_FF = 3584
N_EXPERTS = 8
EPS = 1e-6
NEG = -1e30
LOG2E = 1.4426950408889634

LANES = 128
VMEM_LIMIT = 56 * 1024 * 1024

_IN_SIZES = (256, 128, 32, 256, 256, 256, 256, 256, 256, 64, 64, 64, 64, 64, 64, 12)
_IN_OFF = np.concatenate([[0], np.cumsum(_IN_SIZES)]).tolist()

_W_MLA, _W_CONV, _W_GMLP, _W_NQ, _W_KV, _W_GATE = 640, 768, 512, 256, 128, 128
_IN_SPLITS = (_W_MLA, _W_CONV, _W_GMLP, _W_NQ, _W_KV, _W_KV, _W_KV, _W_GATE)
_D_IN_P = sum(_IN_SPLITS)

_NT = (((1,), (1,)), ((), ()))


def _cparams(sem, vmem=VMEM_LIMIT):
    return pltpu.CompilerParams(dimension_semantics=sem, vmem_limit_bytes=vmem)


def _rms(x, g):
    return x * lax.rsqrt(jnp.mean(x * x, axis=-1, keepdims=True) + EPS) * g


def _dot(a, b):
    return jnp.dot(a, b, preferred_element_type=F32)


def _dot_exact(a, b):
    return jnp.dot(a, b, preferred_element_type=F32, precision=lax.Precision.HIGHEST)


def _gelu(x):
    return 0.5 * x * (1.0 + jnp.tanh(0.7978845608028654 * (x + 0.044715 * (x * x * x))))


def _in_proj_kernel(h_ref, g_ref, w_ref, *o_refs):
    xn = _rms(h_ref[...], g_ref[...]).astype(BF16)
    r = _dot(xn, w_ref[...]).astype(BF16)
    off = 0
    for o_ref, w in zip(o_refs, _IN_SPLITS):
        o_ref[...] = r[:, off:off + w]
        off += w


def in_proj(h, g, w):
    T, D = h.shape
    tm = min(512, T)
    return pl.pallas_call(
        _in_proj_kernel, grid=(T // tm,),
        in_specs=[pl.BlockSpec((tm, D), lambda i: (i, 0)),
                  pl.BlockSpec((1, D), lambda i: (0, 0)),
                  pl.BlockSpec((D, _D_IN_P), lambda i: (0, 0))],
        out_specs=[pl.BlockSpec((tm, w_), lambda i: (i, 0)) for w_ in _IN_SPLITS],
        out_shape=[jax.ShapeDtypeStruct((T, w_), BF16) for w_ in _IN_SPLITS],
        compiler_params=_cparams(("parallel",)), name="in_proj")(h, g, w)


def prep_w_in(w_in):
    o = _IN_OFF
    L, D, _ = w_in.shape
    z = lambda n: jnp.zeros((L, D, n), F32)
    kr = w_in[:, :, o[2]:o[3]]
    half = MLA_ROPE // 2
    kr_sw = jnp.concatenate([kr[..., half:], kr[..., :half]], axis=-1)
    cols = [w_in[:, :, o[0]:o[2]],
            kr, z(96), kr_sw, z(96),
            w_in[:, :, o[3]:o[6]],
            w_in[:, :, o[6]:o[8]],
            w_in[:, :, o[8]:o[9]] * (NSA_DH ** -0.5 * LOG2E),
            w_in[:, :, o[9]:o[15]],
            w_in[:, :, o[15]:o[16]], z(_W_GATE - 12)]
    return jnp.concatenate(cols, axis=-1).astype(BF16)


_MLA_QW = 2 * LANES


def _rope_tables_kernel(pos_ref, freq_ref, sign_ref, c_ref, s_ref):
    ang = pos_ref[...] * freq_ref[...]
    c_ref[...] = jnp.cos(ang)
    s_ref[...] = jnp.sin(ang) * sign_ref[...]


def rope_tables(pos, freq, sign):
    T = pos.shape[0]
    tm = min(1024, T)
    row = pl.BlockSpec((tm, LANES), lambda i: (i, 0))
    one = pl.BlockSpec((1, LANES), lambda i: (0, 0))
    return pl.pallas_call(
        _rope_tables_kernel, grid=(T // tm,),
        in_specs=[pl.BlockSpec((tm, 1), lambda i: (i, 0)), one, one],
        out_specs=[row, row],
        out_shape=[jax.ShapeDtypeStruct((T, LANES), F32), jax.ShapeDtypeStruct((T, LANES), F32)],
        compiler_params=_cparams(("parallel",)), name="rope_tables")(pos, freq, sign)


def _mla_proj_kernel(x_ref, c_ref, s_ref, qg_ref, kvg_ref, wqn_ref, wukt_ref, wra_ref, wrb_ref, q_ref, k_ref):
    x = x_ref[...].astype(F32)
    cq, ckv = x[:, :256], x[:, 256:384]
    kra, krb = x[:, 384:512], x[:, 512:640]
    c, s = c_ref[...], s_ref[...]
    cqn = _rms(cq, qg_ref[...]).astype(BF16)
    qn = _dot(cqn, wqn_ref[...])
    ra = _dot(cqn, wra_ref[...])
    rb = _dot(cqn, wrb_ref[...])
    for h in range(MLA_HEADS):
        hs = slice(h * LANES, (h + 1) * LANES)
        q_ref[:, h * _MLA_QW:h * _MLA_QW + LANES] = _dot(qn[:, hs].astype(BF16), wukt_ref[h]).astype(BF16)
        q_ref[:, h * _MLA_QW + LANES:(h + 1) * _MLA_QW] = (ra[:, hs] * c + rb[:, hs] * s).astype(BF16)
    k_ref[:, :LANES] = _rms(ckv, kvg_ref[...]).astype(BF16)
    k_ref[:, LANES:] = (kra * c + krb * s).astype(BF16)


def mla_proj(x_mla, cos, sin, qg, kvg, wqn, wukt, wra, wrb):
    T = x_mla.shape[0]
    tm = min(512, T)
    full = lambda a: pl.BlockSpec(a.shape, lambda i: (0,) * a.ndim)
    return pl.pallas_call(
        _mla_proj_kernel, grid=(T // tm,),
        in_specs=[pl.BlockSpec((tm, _W_MLA), lambda i: (i, 0)),
                  pl.BlockSpec((tm, LANES), lambda i: (i, 0)),
                  pl.BlockSpec((tm, LANES), lambda i: (i, 0)),
                  full(qg), full(kvg), full(wqn), full(wukt), full(wra), full(wrb)],
        out_specs=[pl.BlockSpec((tm, MLA_HEADS * _MLA_QW), lambda i: (i, 0)),
                   pl.BlockSpec((tm, _MLA_QW), lambda i: (i, 0))],
        out_shape=[jax.ShapeDtypeStruct((T, MLA_HEADS * _MLA_QW), BF16),
                   jax.ShapeDtypeStruct((T, _MLA_QW), BF16)],
        compiler_params=_cparams(("parallel",)), name="mla_proj")(
            x_mla, cos, sin, qg, kvg, wqn, wukt, wra, wrb)


def prep_mla(w_uq, w_ukv):
    L = w_uq.shape[0]
    scale = (MLA_NOPE + MLA_ROPE) ** -0.5 * LOG2E
    half = MLA_ROPE // 2
    hq = MLA_NOPE + MLA_ROPE
    hk = MLA_NOPE + MLA_V
    zq = lambda n: jnp.zeros((L, MLA_Q_RANK, n), F32)
    qn, ra, rb, ukt, uvp = [], [], [], [], []
    for h in range(MLA_HEADS):
        rp = w_uq[:, :, h * hq + MLA_NOPE:(h + 1) * hq]
        qn += [w_uq[:, :, h * hq:h * hq + MLA_NOPE], zq(LANES - MLA_NOPE)]
        ra += [rp, zq(LANES - MLA_ROPE)]
        rb += [jnp.concatenate([rp[..., half:], rp[..., :half]], axis=-1), zq(LANES - MLA_ROPE)]
        w_uk = w_ukv[:, :, h * hk:h * hk + MLA_NOPE]
        ukt.append(jnp.pad(jnp.swapaxes(w_uk, 1, 2), ((0, 0), (0, LANES - MLA_NOPE), (0, 0))))
        w_uv = w_ukv[:, :, h * hk + MLA_NOPE:(h + 1) * hk]
        uvp.append(jnp.pad(w_uv, ((0, 0), (0, 0), (h * MLA_V, (MLA_HEADS - 1 - h) * MLA_V))))
    wqn = (jnp.concatenate(qn, axis=-1) * scale).astype(BF16)
    wra = (jnp.concatenate(ra, axis=-1) * scale).astype(BF16)
    wrb = (jnp.concatenate(rb, axis=-1) * scale).astype(BF16)
    wukt = jnp.stack(ukt, axis=1).astype(BF16)
    wuvp = jnp.stack(uvp, axis=1).astype(BF16)
    inv = ROPE_THETA ** (-jnp.arange(half, dtype=F32) / half)
    pad = jnp.zeros((LANES - MLA_ROPE,), F32)
    freq = jnp.concatenate([inv, inv, pad])[None, :]
    sign = jnp.concatenate([-jnp.ones((half,), F32), jnp.ones((half,), F32), pad])[None, :]
    return wqn, wukt, wra, wrb, wuvp, freq, sign


def _causal_mask(t):
    row = lax.broadcasted_iota(jnp.int32, (t, t), 0)
    col = lax.broadcasted_iota(jnp.int32, (t, t), 1)
    return row >= col


def _lane_fold(x, op):
    r = x[:, :LANES]
    for c in range(1, x.shape[1] // LANES):
        r = op(r, x[:, c * LANES:(c + 1) * LANES])
    return r


class _Softmax:
    def __init__(self, q, m_ref, l_ref, acc_ref):
        self.q, self.m_ref, self.l_ref, self.acc_ref = q, m_ref, l_ref, acc_ref
        m_ref[...] = jnp.full(m_ref.shape, NEG, F32)
        l_ref[...] = jnp.zeros(l_ref.shape, F32)
        acc_ref[...] = jnp.zeros(acc_ref.shape, F32)

    def tile(self, k, v, mask_fn=None):
        s = lax.dot_general(self.q, k, _NT, preferred_element_type=F32)
        if mask_fn is not None:
            s = mask_fn(s)
        m_old = self.m_ref[...]
        m_new = jnp.maximum(m_old, jnp.max(_lane_fold(s, jnp.maximum), axis=-1, keepdims=True))
        alpha = jnp.exp2(m_old - m_new)
        parts = [jnp.exp2(s[:, c * LANES:(c + 1) * LANES] - m_new) for c in range(s.shape[1] // LANES)]
        lsum = parts[0]
        for p in parts[1:]:
            lsum = lsum + p
        self.l_ref[...] = alpha * self.l_ref[...] + lsum
        p = jnp.concatenate(parts, axis=1).astype(BF16)
        self.acc_ref[...] = alpha * self.acc_ref[...] + _dot(p, v)
        self.m_ref[...] = m_new

    def result(self):
        return self.acc_ref[...] / jnp.sum(self.l_ref[...], axis=-1, keepdims=True)


def _local_rc(nh, tq):
    row = lax.broadcasted_iota(jnp.int32, (nh * tq, tq), 0) & (tq - 1)
    col = lax.broadcasted_iota(jnp.int32, (nh * tq, tq), 1)
    return row, col


def _mask_last_block(s, keep, tq):
    w = s.shape[1]
    last = jnp.where(keep, s[:, w - tq:], NEG)
    return last if w == tq else jnp.concatenate([s[:, :w - tq], last], axis=1)


def _causal_sweep(sm, i, nh, tq, tk, load):
    n_full = (i * tq) // tk

    def body(j, c):
        sm.tile(*load(pl.multiple_of(j * tk, tk), tk))
        return c

    lax.fori_loop(0, n_full, body, 0)
    row, col = _local_rc(nh, tq)
    per = tk // tq
    for r in range(per):
        @pl.when(i % per == r)
        def _():
            k, v = load(pl.multiple_of(n_full * tk, tk), (r + 1) * tq)
            sm.tile(k, v, lambda s: _mask_last_block(s, col <= row, tq))


def _mla_attn_kernel(q_ref, k_ref, wuvp_ref, o_ref, qs_ref, m_ref, l_ref, acc_ref, *, tq, tk):
    i = pl.program_id(1)
    for h in range(MLA_HEADS):
        qs_ref[h * tq:(h + 1) * tq, :] = q_ref[:, h * _MLA_QW:(h + 1) * _MLA_QW]
    sm = _Softmax(qs_ref[...], m_ref, l_ref, acc_ref)

    def load(start, width):
        k = k_ref[pl.ds(start, width), :]
        return k, k[:, :LANES]

    _causal_sweep(sm, i, MLA_HEADS, tq, tk, load)
    o_lat = sm.result().astype(BF16)
    out = _dot(o_lat[:tq], wuvp_ref[0])
    for h in range(1, MLA_HEADS):
        out = out + _dot(o_lat[h * tq:(h + 1) * tq], wuvp_ref[h])
    o_ref[...] = out.astype(BF16)


_ATT_TQ, _ATT_TK = 256, 1024


def mla_attn(q, k, wuvp, B, S):
    tq, tk = min(_ATT_TQ, S), min(_ATT_TK, S)
    assert tq & (tq - 1) == 0
    M = MLA_HEADS * tq
    q3, k3 = q.reshape(B, S, MLA_HEADS * _MLA_QW), k.reshape(B, S, _MLA_QW)
    out = pl.pallas_call(
        functools.partial(_mla_attn_kernel, tq=tq, tk=tk), grid=(B, S // tq),
        in_specs=[pl.BlockSpec((None, tq, MLA_HEADS * _MLA_QW), lambda b, i: (b, i, 0)),
                  pl.BlockSpec((None, S, _MLA_QW), lambda b, i: (b, 0, 0)),
                  pl.BlockSpec(wuvp.shape, lambda b, i: (0, 0, 0))],
        out_specs=pl.BlockSpec((None, tq, 256), lambda b, i: (b, i, 0)),
        out_shape=jax.ShapeDtypeStruct((B, S, 256), BF16),
        scratch_shapes=[pltpu.VMEM((M, _MLA_QW), BF16), pltpu.VMEM((M, LANES), F32),
                        pltpu.VMEM((M, LANES), F32), pltpu.VMEM((M, LANES), F32)],
        compiler_params=_cparams(("parallel", "arbitrary")), name="mla_attn")(q3, k3, wuvp)
    return out.reshape(B * S, 256)


def _conv_gmlp_kernel(xc_ref, halo_ref, xg_ref, cw_ref, ln_ref, ws_ref, bs_ref, o_ref, *, tiles_per_seq):
    i = pl.program_id(0)
    tm = xc_ref.shape[0]
    xc = xc_ref[...].astype(F32)
    cb, u = xc[:, :256], xc[:, 256:512] * xc[:, 512:768]
    hx = halo_ref[...].astype(F32)
    keep = jnp.where(i % tiles_per_seq == 0, 0.0, 1.0)
    hu = hx[:, 256:512] * hx[:, 512:768] * keep
    p1, p2 = hu[7:8, :], hu[6:7, :]
    row = lax.broadcasted_iota(jnp.int32, (tm, 256), 0)
    u1 = jnp.where(row == 0, p1, pltpu.roll(u, 1, axis=0))
    u2 = jnp.where(row == 0, p2, jnp.where(row == 1, p1, pltpu.roll(u, 2, axis=0)))
    cw = cw_ref[...]
    y_conv = cb * (cw[0:1, :] * u2 + cw[1:2, :] * u1 + cw[2:3, :] * u)
    o_ref[:, :256] = y_conv.astype(BF16)

    xg = xg_ref[...].astype(F32)
    gu = _gelu(xg[:, :256])
    gv = _gelu(xg[:, 256:512])
    mu = jnp.mean(gv, axis=-1, keepdims=True)
    var = jnp.mean(jnp.square(gv - mu), axis=-1, keepdims=True)
    vn = (gv - mu) * lax.rsqrt(var + EPS) * ln_ref[...]
    lane_g = lax.broadcasted_iota(jnp.int32, (GMLP_CHUNK, 256), 1) // (256 // GMLP_GROUPS)
    tril = _causal_mask(GMLP_CHUNK)
    for c in range(tm // GMLP_CHUNK):
        vc = vn[c * GMLP_CHUNK:(c + 1) * GMLP_CHUNK, :]
        sv = bs_ref[...]
        for g in range(GMLP_GROUPS):
            wg = jnp.where(tril, ws_ref[g], 0.0).astype(BF16)
            sv = sv + _dot(wg, jnp.where(lane_g == g, vc, 0.0).astype(BF16))
        o_ref[c * GMLP_CHUNK:(c + 1) * GMLP_CHUNK, 256:512] = (
            gu[c * GMLP_CHUNK:(c + 1) * GMLP_CHUNK, :] * sv).astype(BF16)


def conv_gmlp(x_conv, x_gmlp, conv_w8, ln_g, w_s, b_exp, S):
    T = x_conv.shape[0]
    tm = min(512, S)
    full = lambda a: pl.BlockSpec(a.shape, lambda i: (0,) * a.ndim)
    return pl.pallas_call(
        functools.partial(_conv_gmlp_kernel, tiles_per_seq=S // tm), grid=(T // tm,),
        in_specs=[pl.BlockSpec((tm, _W_CONV), lambda i: (i, 0)),
                  pl.BlockSpec((8, _W_CONV), lambda i: (jnp.maximum(i * (tm // 8) - 1, 0), 0)),
                  pl.BlockSpec((tm, _W_GMLP), lambda i: (i, 0)),
                  full(conv_w8), full(ln_g), full(w_s), full(b_exp)],
        out_specs=pl.BlockSpec((tm, 512), lambda i: (i, 0)),
        out_shape=jax.ShapeDtypeStruct((T, 512), BF16),
        compiler_params=_cparams(("parallel",)), name="conv_gmlp")(
            x_conv, x_conv, x_gmlp, conv_w8, ln_g, w_s, b_exp)


def _compress_kernel(z_ref, pea_ref, peb_ref, w1a_ref, w1b_ref, w2_ref, o_ref):
    z = z_ref[...]
    n = z.shape[0]
    a = _dot(z, w1a_ref[...])
    b = _dot(z, w1b_ref[...])
    c = _dot_exact(pea_ref[...], w1a_ref[...].astype(F32)) + _dot_exact(peb_ref[...], w1b_ref[...].astype(F32))
    pre = a + pltpu.roll(b, n - 1, axis=0) + c[0:1, :]
    o_ref[...] = _dot(_gelu(pre).astype(BF16), w2_ref[...]).astype(BF16)


def compress(z2, pea, peb, w1a, w1b, w2):
    B, n, K = z2.shape
    full = lambda a: pl.BlockSpec(a.shape, lambda b: (0,) * a.ndim)
    return pl.pallas_call(
        _compress_kernel, grid=(B,),
        in_specs=[pl.BlockSpec((None, n, K), lambda b: (b, 0, 0)),
                  full(pea), full(peb), full(w1a), full(w1b), full(w2)],
        out_specs=pl.BlockSpec((None, n, LANES), lambda b: (b, 0, 0)),
        out_shape=jax.ShapeDtypeStruct((B, n, LANES), BF16),
        compiler_params=_cparams(("parallel",)), name="nsa_compress")(z2, pea, peb, w1a, w1b, w2)


def prep_compress(pe_k, w1_k, w2_k, pe_v, w1_v, w2_v):
    half = CMP_LEN // 2

    def w1_half(w1k, w1v):
        wk = w1k.reshape(half, NSA_DH, CMP_HIDDEN)
        wv = w1v.reshape(half, NSA_DH, CMP_HIDDEN)
        zk = jnp.zeros_like(wk)
        top = jnp.concatenate([wk, zk], axis=-1)
        bot = jnp.concatenate([zk, wv], axis=-1)
        return jnp.concatenate([top, bot], axis=1).reshape(half * 2 * NSA_DH, 2 * CMP_HIDDEN)

    n1 = half * NSA_DH
    w1a = w1_half(w1_k[:n1], w1_v[:n1]).astype(BF16)
    w1b = w1_half(w1_k[n1:], w1_v[n1:]).astype(BF16)

    def pe_half(pk, pv):
        row = jnp.concatenate([pk, pv], axis=-1).reshape(1, half * 2 * NSA_DH)
        return jnp.broadcast_to(row, (8, half * 2 * NSA_DH))

    pea = pe_half(pe_k[:half], pe_v[:half])
    peb = pe_half(pe_k[half:], pe_v[half:])
    z = jnp.zeros((CMP_HIDDEN, NSA_DH), F32)
    w2 = jnp.concatenate([jnp.concatenate([w2_k, z], axis=1),
                          jnp.concatenate([z, w2_v], axis=1)], axis=0).astype(BF16)
    return pea, peb, w1a, w1b, w2


def _head_select(nh):
    sel = np.zeros((nh, nh * NSA_DH, LANES), np.float32)
    place = np.zeros((nh, LANES, nh * NSA_DH), np.float32)
    for h in range(nh):
        for d in range(NSA_DH):
            sel[h, h * NSA_DH + d, d] = 1.0
            place[h, NSA_DH + d, h * NSA_DH + d] = 1.0
    return jnp.asarray(sel, BF16), jnp.asarray(place, BF16)


def _stack_heads(q, sel_ref, qs_ref):
    for h in range(NSA_HEADS):
        qs_ref[h] = _dot(q, sel_ref[h]).astype(BF16)


def _place_heads(o_list, place_ref):
    out = _dot(o_list[0].astype(BF16), place_ref[0])
    for h in range(1, NSA_HEADS):
        out = out + _dot(o_list[h].astype(BF16), place_ref[h])
    return out


def _gate(gl, h, branch):
    c = h * 3 + branch
    return 1.0 / (1.0 + jnp.exp(-gl[:, c:c + 1]))


def _nsa_cmp_kernel(q_ref, kv_ref, gl_ref, msel_ref, sel_ref, place_ref, o_ref, bias_ref, qs_ref, *, t, ns):
    i = pl.program_id(1)
    nc = kv_ref.shape[0]
    _stack_heads(q_ref[...], sel_ref, qs_ref)
    kv = kv_ref[...]
    gl = gl_ref[...].astype(F32)
    tpos = i * t + lax.broadcasted_iota(jnp.int32, (t, nc), 0)
    blk_end = lax.broadcasted_iota(jnp.int32, (t, nc), 1) * CMP_STRIDE + (CMP_LEN - 1)
    cmask = blk_end <= tpos
    p_sum = jnp.zeros((t, nc), F32)
    outs = []
    for h in range(NSA_HEADS):
        s = lax.dot_general(qs_ref[h], kv, _NT, preferred_element_type=F32)
        s = jnp.where(cmask, s, NEG)
        m = jnp.max(s, axis=-1, keepdims=True)
        e = jnp.where(cmask, jnp.exp2(s - m), 0.0)
        l = jnp.sum(e, axis=-1, keepdims=True)
        p = e / jnp.where(l > 0.0, l, 1.0)
        p_sum = p_sum + p
        outs.append(_gate(gl, h, 0) * _dot(p.astype(BF16), kv))
    o_ref[...] = _place_heads(outs, place_ref).astype(BF16)

    p_sel = _dot_exact(p_sum, msel_ref[...])
    blk = lax.broadcasted_iota(jnp.int32, (t, LANES), 1) - NSA_DH
    cur = (i * t + lax.broadcasted_iota(jnp.int32, (t, LANES), 0)) // SEL_LEN
    forced = (blk == 0) | (blk == cur) | (blk == cur - 1)
    v = jnp.where(forced, FORCE_SCORE, jnp.where(blk <= cur, p_sel, -1.0))
    v = jnp.where((blk >= 0) & (blk < ns), v, -jnp.inf)
    vt = v.T[NSA_DH:, :]
    bt = lax.broadcasted_iota(jnp.int32, vt.shape, 0).astype(F32)
    chosen = jnp.zeros(vt.shape, jnp.bool_)
    for _ in range(min(N_SELECT, ns)):
        mx = jnp.max(vt, axis=0, keepdims=True)
        idx = jnp.min(jnp.where(vt == mx, bt, float(LANES)), axis=0, keepdims=True)
        hit = bt == idx
        chosen = chosen | hit
        vt = jnp.where(hit, -jnp.inf, vt)
    bias_t = jnp.concatenate([jnp.zeros(vt.shape, F32), jnp.where(chosen, 0.0, NEG)], axis=0)
    bias_ref[...] = bias_t.T.astype(BF16)


def nsa_cmp(q, kcv, gl, msel, sel, place, B, S):
    t = min(1024, S)
    nc = kcv.shape[1]
    ns = S // SEL_LEN
    full = lambda a: pl.BlockSpec(a.shape, lambda b, i: (0,) * a.ndim)
    return pl.pallas_call(
        functools.partial(_nsa_cmp_kernel, t=t, ns=ns), grid=(B, S // t),
        in_specs=[pl.BlockSpec((None, t, 256), lambda b, i: (b, i, 0)),
                  pl.BlockSpec((None, nc, LANES), lambda b, i: (b, 0, 0)),
                  pl.BlockSpec((None, t, LANES), lambda b, i: (b, i, 0)),
                  full(msel), full(sel), full(place)],
        out_specs=[pl.BlockSpec((None, t, 256), lambda b, i: (b, i, 0)),
                   pl.BlockSpec((None, t, LANES), lambda b, i: (b, i, 0))],
        out_shape=[jax.ShapeDtypeStruct((B, S, 256), BF16), jax.ShapeDtypeStruct((B, S, LANES), BF16)],
        scratch_shapes=[pltpu.VMEM((NSA_HEADS, t, LANES), BF16)],
        compiler_params=_cparams(("parallel", "arbitrary")), name="nsa_cmp")(
            q.reshape(B, S, 256), kcv, gl.reshape(B, S, LANES), msel, sel, place)


def cmp_to_sel(S):
    nc = S // CMP_STRIDE - CMP_LEN // CMP_STRIDE + 1
    ns = S // SEL_LEN
    cs = np.arange(nc)[:, None] * CMP_STRIDE
    ss = np.arange(ns)[None, :] * SEL_LEN
    ov = np.clip(np.minimum(cs + CMP_LEN, ss + SEL_LEN) - np.maximum(cs, ss), 0, None)
    assert ns <= LANES - NSA_DH, "selection blocks must fit the lanes beside one head's query"
    m = np.zeros((S // CMP_STRIDE, LANES), np.float32)
    m[:nc, NSA_DH:NSA_DH + ns] = ov.astype(np.float32) / np.float32(CMP_LEN)
    return jnp.asarray(m)


def _nsa_attn_kernel(q_ref, ksel_ref, kwin_ref, bias_ref, ocmp_ref, gl_ref, sel_ref, place_ref,
                     o_ref, kx_ref, kwp_ref, qs_ref, qw_ref, m_ref, l_ref, acc_ref, *, tq, tk):
    i = pl.program_id(1)
    S = ksel_ref.shape[0]

    @pl.when(i == 0)
    def _():
        r = lax.broadcasted_iota(jnp.int32, (S, LANES), 0)
        c = lax.broadcasted_iota(jnp.int32, (S, LANES), 1)
        onehot = jnp.where(r // SEL_LEN == c - NSA_DH, 1.0, 0.0)
        kx_ref[...] = jnp.where(c < NSA_DH, ksel_ref[...].astype(F32), onehot).astype(BF16)
        kwp_ref[:WINDOW, :] = jnp.zeros((WINDOW, LANES), BF16)
        kwp_ref[WINDOW:, :] = kwin_ref[...]

    q = q_ref[...]
    bias = bias_ref[...].astype(F32)
    for h in range(NSA_HEADS):
        qs = _dot(q, sel_ref[h])
        qw_ref[h * tq:(h + 1) * tq, :] = qs.astype(BF16)
        qs_ref[h * tq:(h + 1) * tq, :] = (qs + bias).astype(BF16)

    def heads(x):
        return [x[h * tq:(h + 1) * tq] for h in range(NSA_HEADS)]

    sm = _Softmax(qs_ref[...], m_ref, l_ref, acc_ref)
    _causal_sweep(sm, i, NSA_HEADS, tq, tk,
                  lambda start, width: (kx_ref[pl.ds(start, width), :], ksel_ref[pl.ds(start, width), :]))
    o_sel = heads(sm.result())

    sm = _Softmax(qw_ref[...], m_ref, l_ref, acc_ref)
    row, col = _local_rc(NSA_HEADS, tq)
    nblk = WINDOW // tq + 1

    def win_mask(s):
        blocks = []
        for c in range(nblk):
            sc = s[:, c * tq:(c + 1) * tq]
            if c == 0:
                sc = jnp.where(row < col, sc, NEG)
            if c == nblk - 1:
                sc = jnp.where(col <= row, sc, NEG)
            else:
                sc = sc + jnp.where(i >= nblk - 1 - c, 0.0, NEG)
            blocks.append(sc)
        return jnp.concatenate(blocks, axis=1)

    kw = kwp_ref[pl.ds(pl.multiple_of(i * tq, tq), WINDOW + tq), :]
    sm.tile(kw, kw, win_mask)
    o_win = heads(sm.result())

    gl = gl_ref[...].astype(F32)
    gated = [_gate(gl, h, 1) * o_sel[h] + _gate(gl, h, 2) * o_win[h] for h in range(NSA_HEADS)]
    o_ref[...] = (ocmp_ref[...].astype(F32) + _place_heads(gated, place_ref)).astype(BF16)


def nsa_attn(q, ksel, kwin, bias, o_cmp, gl, sel, place, B, S):
    tq, tk = min(_ATT_TQ, S), min(_ATT_TK, S)
    assert WINDOW % tq == 0
    M = NSA_HEADS * tq
    full = lambda a: pl.BlockSpec(a.shape, lambda b, i: (0,) * a.ndim)
    tile2 = lambda w: pl.BlockSpec((None, tq, w), lambda b, i: (b, i, 0))
    seq = pl.BlockSpec((None, S, LANES), lambda b, i: (b, 0, 0))
    out = pl.pallas_call(
        functools.partial(_nsa_attn_kernel, tq=tq, tk=tk), grid=(B, S // tq),
        in_specs=[tile2(256), seq, seq, tile2(LANES), tile2(256), tile2(LANES), full(sel), full(place)],
        out_specs=tile2(256),
        out_shape=jax.ShapeDtypeStruct((B, S, 256), BF16),
        scratch_shapes=[pltpu.VMEM((S, LANES), BF16), pltpu.VMEM((S + WINDOW, LANES), BF16),
                        pltpu.VMEM((M, LANES), BF16), pltpu.VMEM((M, LANES), BF16),
                        pltpu.VMEM((M, LANES), F32), pltpu.VMEM((M, LANES), F32), pltpu.VMEM((M, LANES), F32)],
        compiler_params=_cparams(("parallel", "arbitrary")), name="nsa_attn")(
            q.reshape(B, S, 256), ksel.reshape(B, S, LANES), kwin.reshape(B, S, LANES), bias, o_cmp,
            gl.reshape(B, S, LANES), sel, place)
    return out.reshape(B * S, 256)


def _mix_out_kernel(ymla_ref, ycg_ref, ynsa_ref, h_ref, gn_ref, wo_ref, o_ref):
    gn = gn_ref[...]
    parts = [ymla_ref[...].astype(F32), ycg_ref[:, :256].astype(F32), ycg_ref[:, 256:].astype(F32),
             ynsa_ref[...].astype(F32)]
    yn = [_rms(p, gn[:, k * GROUP_W:(k + 1) * GROUP_W]).astype(BF16) for k, p in enumerate(parts)]
    o_ref[...] = h_ref[...] + _dot(jnp.concatenate(yn, axis=1), wo_ref[...])


def mix_out(y_mla, y_cg, y_nsa, h, gn, wo):
    T, D = h.shape
    tm = min(512, T)
    row = lambda w: pl.BlockSpec((tm, w), lambda i: (i, 0))
    return pl.pallas_call(
        _mix_out_kernel, grid=(T // tm,),
        in_specs=[row(256), row(512), row(256), row(D),
                  pl.BlockSpec((1, D), lambda i: (0, 0)), pl.BlockSpec((D, D), lambda i: (0, 0))],
        out_specs=row(D), out_shape=jax.ShapeDtypeStruct((T, D), F32),
        compiler_params=_cparams(("parallel",)), name="mix_out")(y_mla, y_cg, y_nsa, h, gn, wo)


def _ffn_step(j, nj, x_ref, g_ref, wg_ref, wu_ref, wd_ref, o_ref, hn_ref, acc_ref, residual):
    @pl.when(j == 0)
    def _():
        hn_ref[...] = _rms(x_ref[...], g_ref[...]).astype(BF16)
        acc_ref[...] = jnp.zeros(acc_ref.shape, F32)

    hn = hn_ref[...]
    a = _dot(hn, wg_ref[...])
    b = _dot(hn, wu_ref[...])
    act = (a / (1.0 + jnp.exp(-a)) * b).astype(BF16)
    acc_ref[...] += _dot(act, wd_ref[...])

    @pl.when(j == nj - 1)
    def _():
        o_ref[...] = (x_ref[...] + acc_ref[...]) if residual else acc_ref[...]


def _ffn_kernel(x_ref, g_ref, wg_ref, wu_ref, wd_ref, o_ref, hn_ref, acc_ref):
    _ffn_step(pl.program_id(1), pl.num_programs(1), x_ref, g_ref, wg_ref, wu_ref, wd_ref, o_ref, hn_ref,
              acc_ref, True)


_FFN_TF = 512


def ffn_dense(h, g, wg, wu, wd):
    T, D = h.shape
    F = wg.shape[1]
    tm = min(1024, T)
    return pl.pallas_call(
        _ffn_kernel, grid=(T // tm, F // _FFN_TF),
        in_specs=[pl.BlockSpec((tm, D), lambda i, j: (i, 0)),
                  pl.BlockSpec((1, D), lambda i, j: (0, 0)),
                  pl.BlockSpec((D, _FFN_TF), lambda i, j: (0, j)),
                  pl.BlockSpec((D, _FFN_TF), lambda i, j: (0, j)),
                  pl.BlockSpec((_FFN_TF, D), lambda i, j: (j, 0))],
        out_specs=pl.BlockSpec((tm, D), lambda i, j: (i, 0)),
        out_shape=jax.ShapeDtypeStruct((T, D), F32),
        scratch_shapes=[pltpu.VMEM((tm, D), BF16), pltpu.VMEM((tm, D), F32)],
        compiler_params=_cparams(("parallel", "arbitrary")), name="ffn_dense")(h, g, wg, wu, wd)


def _ffn_grouped_kernel(te_ref, nu_ref, x_ref, g_ref, wg_ref, wu_ref, wd_ref, o_ref, hn_ref, acc_ref):
    del te_ref
    i, j, nj = pl.program_id(0), pl.program_id(1), pl.num_programs(1)
    used = i < nu_ref[0]

    @pl.when(used)
    def _():
        _ffn_step(j, nj, x_ref, g_ref, wg_ref, wu_ref, wd_ref, o_ref, hn_ref, acc_ref, False)

    @pl.when(jnp.logical_not(used))
    def _():
        o_ref[...] = jnp.zeros(o_ref.shape, F32)


def ffn_grouped(xs, g, wg, wu, wd, tile_expert, n_used, tm):
    NP, D = xs.shape
    F = wg.shape[2]
    nf = F // _FFN_TF

    def jj(i, j, nu):
        return jnp.where(i < nu[0], j, nf - 1)

    grid_spec = pltpu.PrefetchScalarGridSpec(
        num_scalar_prefetch=2, grid=(NP // tm, nf),
        in_specs=[pl.BlockSpec((tm, D), lambda i, j, te, nu: (i, 0)),
                  pl.BlockSpec((1, D), lambda i, j, te, nu: (0, 0)),
                  pl.BlockSpec((None, D, _FFN_TF), lambda i, j, te, nu: (te[i], 0, jj(i, j, nu))),
                  pl.BlockSpec((None, D, _FFN_TF), lambda i, j, te, nu: (te[i], 0, jj(i, j, nu))),
                  pl.BlockSpec((None, _FFN_TF, D), lambda i, j, te, nu: (te[i], jj(i, j, nu), 0))],
        out_specs=pl.BlockSpec((tm, D), lambda i, j, te, nu: (i, 0)),
        scratch_shapes=[pltpu.VMEM((tm, D), BF16), pltpu.VMEM((tm, D), F32)])
    return pl.pallas_call(
        _ffn_grouped_kernel, grid_spec=grid_spec,
        out_shape=jax.ShapeDtypeStruct((NP, D), F32),
        compiler_params=_cparams(("arbitrary", "arbitrary")), name="ffn_grouped")(
            tile_expert, n_used, xs, g, wg, wu, wd)


def _router_kernel(h_ref, g_ref, wr_ref, o_ref):
    hn = _rms(h_ref[...], g_ref[...])
    logits = _dot_exact(hn, wr_ref[...])
    tm = logits.shape[0]
    lane = lax.broadcasted_iota(jnp.int32, (tm, LANES), 1)
    lane_f = lane.astype(F32)
    lg = jnp.where(lane < N_EXPERTS, logits, -jnp.inf)
    m1 = jnp.max(lg, axis=-1, keepdims=True)
    i1 = jnp.min(jnp.where(lg == m1, lane_f, float(LANES)), axis=-1, keepdims=True)
    lg2 = jnp.where(lane_f == i1, -jnp.inf, lg)
    m2 = jnp.max(lg2, axis=-1, keepdims=True)
    i2 = jnp.min(jnp.where(lg2 == m2, lane_f, float(LANES)), axis=-1, keepdims=True)
    e = jnp.exp(m2 - m1)
    w1 = 1.0 / (1.0 + e)
    w2 = e / (1.0 + e)
    o_ref[...] = jnp.where(lane == 0, i1,
                           jnp.where(lane == 1, i2,
                                     jnp.where(lane == 2, w1, jnp.where(lane == 3, w2, 0.0))))


def router(h, g, wr):
    T, D = h.shape
    tm = min(512, T)
    return pl.pallas_call(
        _router_kernel, grid=(T // tm,),
        in_specs=[pl.BlockSpec((tm, D), lambda i: (i, 0)), pl.BlockSpec((1, D), lambda i: (0, 0)),
                  pl.BlockSpec((D, LANES), lambda i: (0, 0))],
        out_specs=pl.BlockSpec((tm, LANES), lambda i: (i, 0)),
        out_shape=jax.ShapeDtypeStruct((T, LANES), F32),
        compiler_params=_cparams(("parallel",)), name="moe_router")(h, g, wr)


def _row_copy(src, s, dst, d, sem):
    return pltpu.make_async_copy(src.at[pl.ds(s, 1)], dst.at[pl.ds(d, 1)], sem)


def _dispatch_kernel(dest_ref, h_ref, xs_in_ref, xs_ref, sem):
    del xs_in_ref
    tm = h_ref.shape[0]

    def issue(r, c):
        _row_copy(h_ref, r, xs_ref, dest_ref[0, 0, 2 * r], sem).start()
        _row_copy(h_ref, r, xs_ref, dest_ref[0, 0, 2 * r + 1], sem).start()
        return c

    lax.fori_loop(0, tm, issue, 0, unroll=8)
    for _ in range(2):
        pltpu.make_async_copy(h_ref, xs_ref.at[pl.ds(0, tm)], sem).wait()


def dispatch(h, dest, xs_zero):
    T, D = h.shape
    tm = min(512, T)
    dest3 = dest.reshape(T // tm, 1, 2 * tm)
    return pl.pallas_call(
        _dispatch_kernel, grid=(T // tm,),
        in_specs=[pl.BlockSpec((1, 1, 2 * tm), lambda i: (i, 0, 0), memory_space=pltpu.SMEM),
                  pl.BlockSpec((tm, D), lambda i: (i, 0)),
                  pl.BlockSpec(memory_space=pl.ANY)],
        out_specs=pl.BlockSpec(memory_space=pl.ANY),
        out_shape=jax.ShapeDtypeStruct(xs_zero.shape, F32),
        scratch_shapes=[pltpu.SemaphoreType.DMA(())],
        input_output_aliases={2: 0},
        compiler_params=_cparams(("arbitrary",)), name="moe_dispatch")(dest3, h, xs_zero)


def _combine_kernel(dest_ref, dnext_ref, h_ref, route_ref, *rest, out_norm):
    og_ref = rest[0] if out_norm else None
    ys_ref, o_ref, b0_ref, b1_ref, sem = rest[1:] if out_norm else rest
    i, n = pl.program_id(0), pl.num_programs(0)
    tm = h_ref.shape[0]
    slot = i % 2

    def gather(d_ref, s):
        def issue(r, c):
            _row_copy(ys_ref, d_ref[0, 0, 2 * r], b0_ref.at[s], r, sem.at[s]).start()
            _row_copy(ys_ref, d_ref[0, 0, 2 * r + 1], b1_ref.at[s], r, sem.at[s]).start()
            return c

        lax.fori_loop(0, tm, issue, 0, unroll=8)

    @pl.when(i == 0)
    def _():
        gather(dest_ref, 0)

    @pl.when(i + 1 < n)
    def _():
        gather(dnext_ref, 1 - slot)

    pltpu.make_async_copy(ys_ref.at[pl.ds(0, tm)], b0_ref.at[slot], sem.at[slot]).wait()
    pltpu.make_async_copy(ys_ref.at[pl.ds(0, tm)], b1_ref.at[slot], sem.at[slot]).wait()
    route = route_ref[...]
    out = h_ref[...] + route[:, 2:3] * b0_ref[slot] + route[:, 3:4] * b1_ref[slot]
    o_ref[...] = _rms(out, og_ref[...]) if out_norm else out


def combine(h, route, dest, ys, out_g=None):
    T, D = h.shape
    tm = min(256, T)
    n = T // tm
    dest3 = dest.reshape(n, 1, 2 * tm)
    smem = lambda f: pl.BlockSpec((1, 1, 2 * tm), f, memory_space=pltpu.SMEM)
    out_norm = out_g is not None
    norm_spec = [pl.BlockSpec((1, D), lambda i: (0, 0))] if out_norm else []
    norm_arg = [out_g] if out_norm else []
    return pl.pallas_call(
        functools.partial(_combine_kernel, out_norm=out_norm), grid=(n,),
        in_specs=[smem(lambda i: (i, 0, 0)), smem(lambda i: (jnp.minimum(i + 1, n - 1), 0, 0)),
                  pl.BlockSpec((tm, D), lambda i: (i, 0)),
                  pl.BlockSpec((tm, LANES), lambda i: (i, 0))] + norm_spec +
                 [pl.BlockSpec(memory_space=pl.ANY)],
        out_specs=pl.BlockSpec((tm, D), lambda i: (i, 0)),
        out_shape=jax.ShapeDtypeStruct((T, D), F32),
        scratch_shapes=[pltpu.VMEM((2, tm, D), F32), pltpu.VMEM((2, tm, D), F32),
                        pltpu.SemaphoreType.DMA((2,))],
        compiler_params=_cparams(("arbitrary",)), name="moe_combine")(dest3, dest3, h, route, *norm_arg, ys)


def moe_ffn(h, g, wr, wg, wu, wd, out_g=None):
    T, D = h.shape
    tm = min(1024, T)
    route = router(h, g, wr)
    e_flat = route[:, :2].astype(jnp.int32).reshape(2 * T)
    onehot = (e_flat[:, None] == jnp.arange(N_EXPERTS, dtype=jnp.int32)[None, :]).astype(jnp.int32)
    csum = jnp.cumsum(onehot, axis=0)
    counts = csum[-1]
    rank = jnp.sum((csum - 1) * onehot, axis=1)
    tiles = (counts + tm - 1) // tm
    tile_end = jnp.cumsum(tiles)
    offs = (tile_end - tiles) * tm
    dest = (jnp.sum(onehot * offs[None, :], axis=1) + rank).astype(jnp.int32)
    n_tiles = (2 * T) // tm + N_EXPERTS
    n_used = tile_end[-1].astype(jnp.int32)
    tile_ids = jnp.minimum(jnp.arange(n_tiles, dtype=jnp.int32), n_used - 1)
    tile_expert = jnp.sum((tile_ids[:, None] >= tile_end[None, :]).astype(jnp.int32), axis=1).astype(jnp.int32)
    xs = dispatch(h, dest, jnp.zeros((n_tiles * tm, D), F32))
    ys = ffn_grouped(xs, g, wg, wu, wd, tile_expert, n_used.reshape(1), tm)
    return combine(h, route, dest, ys, out_g)


def _final_norm_kernel(h_ref, g_ref, o_ref):
    o_ref[...] = _rms(h_ref[...], g_ref[...])


def final_rms(h, g):
    T, D = h.shape
    tm = min(1024, T)
    return pl.pallas_call(
        _final_norm_kernel, grid=(T // tm,),
        in_specs=[pl.BlockSpec((tm, D), lambda i: (i, 0)), pl.BlockSpec((1, D), lambda i: (0, 0))],
        out_specs=pl.BlockSpec((tm, D), lambda i: (i, 0)),
        out_shape=jax.ShapeDtypeStruct((T, D), F32),
        compiler_params=_cparams(("parallel",)), name="final_norm")(h, g)


def kernel(x, positions, attn_norm, w_in, mla_q_norm, mla_w_uq, mla_kv_norm, mla_w_ukv, conv_w, gmlp_ln, gmlp_w_s, gmlp_b_s, nsa_pe_k, nsa_w1_k, nsa_w2_k, nsa_pe_v, nsa_w1_v, nsa_w2_v, group_norm, w_o, ffn_norm, dense_w_gate, dense_w_up, dense_w_down, moe_router, moe_w_gate, moe_w_up, moe_w_down, final_norm):
    B, S, D = x.shape
    T = B * S
    depth = w_in.shape[0]
    h = x.reshape(T, D)
    pos = positions.astype(F32).reshape(T, 1)

    w_in_p = prep_w_in(w_in)
    wqn, wukt, wra, wrb, wuvp, freq, sign = prep_mla(mla_w_uq, mla_w_ukv)
    rope_cos, rope_sin = rope_tables(pos, freq, sign)
    conv_w8 = jnp.pad(conv_w, ((0, 0), (0, 8 - CONV_K), (0, 0)))
    b_exp = jnp.repeat(jnp.swapaxes(gmlp_b_s, 1, 2), GROUP_W // GMLP_GROUPS, axis=2)
    sel, place = _head_select(NSA_HEADS)
    msel = cmp_to_sel(S)
    w_o_b = w_o.astype(BF16)
    wr_p = jnp.pad(moe_router, ((0, 0), (0, 0), (0, LANES - N_EXPERTS)))

    for l in range(depth):
        x_mla, x_conv, x_gmlp, x_nq, x_cmp, x_sel, x_win, x_gate = in_proj(h, attn_norm[l][None, :], w_in_p[l])

        q, k = mla_proj(x_mla, rope_cos, rope_sin, mla_q_norm[l][None, :], mla_kv_norm[l][None, :],
                        wqn[l], wukt[l], wra[l], wrb[l])
        y_mla = mla_attn(q, k, wuvp[l], B, S)

        y_cg = conv_gmlp(x_conv, x_gmlp, conv_w8[l], gmlp_ln[l][None, :], gmlp_w_s[l], b_exp[l], S)

        pea, peb, w1a, w1b, w2 = prep_compress(nsa_pe_k[l], nsa_w1_k[l], nsa_w2_k[l],
                                               nsa_pe_v[l], nsa_w1_v[l], nsa_w2_v[l])
        z2 = x_cmp.reshape(B, S // CMP_STRIDE, CMP_STRIDE * LANES)
        kcv = compress(z2, pea, peb, w1a, w1b, w2)
        o_cmp, sel_bias = nsa_cmp(x_nq, kcv, x_gate, msel, sel, place, B, S)
        y_nsa = nsa_attn(x_nq, x_sel, x_win, sel_bias, o_cmp, x_gate, sel, place, B, S)

        h = mix_out(y_mla, y_cg, y_nsa, h, group_norm[l].reshape(1, D), w_o_b[l])

        fg = ffn_norm[l][None, :]
        if l % 2 == 0:
            h = ffn_dense(h, fg, dense_w_gate[l // 2].astype(BF16), dense_w_up[l // 2].astype(BF16),
                          dense_w_down[l // 2].astype(BF16))
        else:
            out_g = final_norm[None, :] if l == depth - 1 else None
            h = moe_ffn(h, fg, wr_p[l // 2], moe_w_gate[l // 2].astype(BF16), moe_w_up[l // 2].astype(BF16),
                        moe_w_down[l // 2].astype(BF16), out_g)
    if depth % 2 == 1:
        h = final_rms(h, final_norm[None, :])
    return h.reshape(B, S, D)
```

```python
import functools

import numpy as np
import jax
import jax.numpy as jnp
from jax import lax
from jax.experimental import pallas as pl
from jax.experimental.pallas import tpu as pltpu

F32 = jnp.float32
BF16 = jnp.bfloat16

D_MODEL = 1024
GROUP_W = 256
MLA_HEADS = 4
MLA_NOPE = 64
MLA_ROPE = 32
MLA_V = 64
MLA_Q_RANK = 256
MLA_KV_RANK = 128
ROPE_THETA = 10000.0
CONV_K = 3
GMLP_GROUPS = 4
GMLP_CHUNK = 128
NSA_HEADS = 4
NSA_DH = 64
CMP_LEN = 32
CMP_STRIDE = 16
CMP_HIDDEN = 256
SEL_LEN = 64
N_SELECT = 16
WINDOW = 512
FORCE_SCORE = 1e4
D_FF = 3584
N_EXPERTS = 8
EPS = 1e-6
NEG = -1e30
LOG2E = 1.4426950408889634

LANES = 128
VMEM_LIMIT = 56 * 1024 * 1024

_IN_SIZES = (256, 128, 32, 256, 256, 256, 256, 256, 256, 64, 64, 64, 64, 64, 64, 12)
_IN_OFF = np.concatenate([[0], np.cumsum(_IN_SIZES)]).tolist()

_W_MLA, _W_CONV, _W_GMLP, _W_NQ, _W_KV, _W_GATE = 640, 768, 512, 256, 128, 128
_IN_SPLITS = (_W_MLA, _W_CONV, _W_GMLP, _W_NQ, _W_KV, _W_KV, _W_KV, _W_GATE)
_D_IN_P = sum(_IN_SPLITS)

_NT = (((1,), (1,)), ((), ()))


def _cparams(sem, vmem=VMEM_LIMIT):
    return pltpu.CompilerParams(dimension_semantics=sem, vmem_limit_bytes=vmem)


def _rms(x, g):
    return x * lax.rsqrt(jnp.mean(x * x, axis=-1, keepdims=True) + EPS) * g


def _dot(a, b):
    return jnp.dot(a, b, preferred_element_type=F32)


def _dot_exact(a, b):
    return jnp.dot(a, b, preferred_element_type=F32, precision=lax.Precision.HIGHEST)


def _gelu(x):
    return 0.5 * x * (1.0 + jnp.tanh(0.7978845608028654 * (x + 0.044715 * (x * x * x))))


def _in_proj_kernel(h_ref, g_ref, w_ref, *o_refs):
    xn = _rms(h_ref[...], g_ref[...]).astype(BF16)
    r = _dot(xn, w_ref[...]).astype(BF16)
    off = 0
    for o_ref, w in zip(o_refs, _IN_SPLITS):
        o_ref[...] = r[:, off:off + w]
        off += w


def in_proj(h, g, w):
    T, D = h.shape
    tm = min(512, T)
    return pl.pallas_call(
        _in_proj_kernel, grid=(T // tm,),
        in_specs=[pl.BlockSpec((tm, D), lambda i: (i, 0)),
                  pl.BlockSpec((1, D), lambda i: (0, 0)),
                  pl.BlockSpec((D, _D_IN_P), lambda i: (0, 0))],
        out_specs=[pl.BlockSpec((tm, w_), lambda i: (i, 0)) for w_ in _IN_SPLITS],
        out_shape=[jax.ShapeDtypeStruct((T, w_), BF16) for w_ in _IN_SPLITS],
        compiler_params=_cparams(("parallel",)), name="in_proj")(h, g, w)


def prep_w_in(w_in):
    o = _IN_OFF
    L, D, _ = w_in.shape
    z = lambda n: jnp.zeros((L, D, n), F32)
    kr = w_in[:, :, o[2]:o[3]]
    half = MLA_ROPE // 2
    kr_sw = jnp.concatenate([kr[..., half:], kr[..., :half]], axis=-1)
    cols = [w_in[:, :, o[0]:o[2]],
            kr, z(96), kr_sw, z(96),
            w_in[:, :, o[3]:o[6]],
            w_in[:, :, o[6]:o[8]],
            w_in[:, :, o[8]:o[9]] * (NSA_DH ** -0.5 * LOG2E),
            w_in[:, :, o[9]:o[15]],
            w_in[:, :, o[15]:o[16]], z(_W_GATE - 12)]
    return jnp.concatenate(cols, axis=-1).astype(BF16)


_MLA_QW = 2 * LANES


def _rope_tables_kernel(pos_ref, freq_ref, sign_ref, c_ref, s_ref):
    ang = pos_ref[...] * freq_ref[...]
    c_ref[...] = jnp.cos(ang)
    s_ref[...] = jnp.sin(ang) * sign_ref[...]


def rope_tables(pos, freq, sign):
    T = pos.shape[0]
    tm = min(1024, T)
    row = pl.BlockSpec((tm, LANES), lambda i: (i, 0))
    one = pl.BlockSpec((1, LANES), lambda i: (0, 0))
    return pl.pallas_call(
        _rope_tables_kernel, grid=(T // tm,),
        in_specs=[pl.BlockSpec((tm, 1), lambda i: (i, 0)), one, one],
        out_specs=[row, row],
        out_shape=[jax.ShapeDtypeStruct((T, LANES), F32), jax.ShapeDtypeStruct((T, LANES), F32)],
        compiler_params=_cparams(("parallel",)), name="rope_tables")(pos, freq, sign)


def _mla_proj_kernel(x_ref, c_ref, s_ref, qg_ref, kvg_ref, wqn_ref, wukt_ref, wra_ref, wrb_ref, q_ref, k_ref):
    x = x_ref[...].astype(F32)
    cq, ckv = x[:, :256], x[:, 256:384]
    kra, krb = x[:, 384:512], x[:, 512:640]
    c, s = c_ref[...], s_ref[...]
    cqn = _rms(cq, qg_ref[...]).astype(BF16)
    qn = _dot(cqn, wqn_ref[...])
    ra = _dot(cqn, wra_ref[...])
    rb = _dot(cqn, wrb_ref[...])
    for h in range(MLA_HEADS):
        hs = slice(h * LANES, (h + 1) * LANES)
        q_ref[:, h * _MLA_QW:h * _MLA_QW + LANES] = _dot(qn[:, hs].astype(BF16), wukt_ref[h]).astype(BF16)
        q_ref[:, h * _MLA_QW + LANES:(h + 1) * _MLA_QW] = (ra[:, hs] * c + rb[:, hs] * s).astype(BF16)
    k_ref[:, :LANES] = _rms(ckv, kvg_ref[...]).astype(BF16)
    k_ref[:, LANES:] = (kra * c + krb * s).astype(BF16)


def mla_proj(x_mla, cos, sin, qg, kvg, wqn, wukt, wra, wrb):
    T = x_mla.shape[0]
    tm = min(512, T)
    full = lambda a: pl.BlockSpec(a.shape, lambda i: (0,) * a.ndim)
    return pl.pallas_call(
        _mla_proj_kernel, grid=(T // tm,),
        in_specs=[pl.BlockSpec((tm, _W_MLA), lambda i: (i, 0)),
                  pl.BlockSpec((tm, LANES), lambda i: (i, 0)),
                  pl.BlockSpec((tm, LANES), lambda i: (i, 0)),
                  full(qg), full(kvg), full(wqn), full(wukt), full(wra), full(wrb)],
        out_specs=[pl.BlockSpec((tm, MLA_HEADS * _MLA_QW), lambda i: (i, 0)),
                   pl.BlockSpec((tm, _MLA_QW), lambda i: (i, 0))],
        out_shape=[jax.ShapeDtypeStruct((T, MLA_HEADS * _MLA_QW), BF16),
                   jax.ShapeDtypeStruct((T, _MLA_QW), BF16)],
        compiler_params=_cparams(("parallel",)), name="mla_proj")(
            x_mla, cos, sin, qg, kvg, wqn, wukt, wra, wrb)


def prep_mla(w_uq, w_ukv):
    L = w_uq.shape[0]
    scale = (MLA_NOPE + MLA_ROPE) ** -0.5 * LOG2E
    half = MLA_ROPE // 2
    hq = MLA_NOPE + MLA_ROPE
    hk = MLA_NOPE + MLA_V
    zq = lambda n: jnp.zeros((L, MLA_Q_RANK, n), F32)
    qn, ra, rb, ukt, uvp = [], [], [], [], []
    for h in range(MLA_HEADS):
        rp = w_uq[:, :, h * hq + MLA_NOPE:(h + 1) * hq]
        qn += [w_uq[:, :, h * hq:h * hq + MLA_NOPE], zq(LANES - MLA_NOPE)]
        ra += [rp, zq(LANES - MLA_ROPE)]
        rb += [jnp.concatenate([rp[..., half:], rp[..., :half]], axis=-1), zq(LANES - MLA_ROPE)]
        w_uk = w_ukv[:, :, h * hk:h * hk + MLA_NOPE]
        ukt.append(jnp.pad(jnp.swapaxes(w_uk, 1, 2), ((0, 0), (0, LANES - MLA_NOPE), (0, 0))))
        w_uv = w_ukv[:, :, h * hk + MLA_NOPE:(h + 1) * hk]
        uvp.append(jnp.pad(w_uv, ((0, 0), (0, 0), (h * MLA_V, (MLA_HEADS - 1 - h) * MLA_V))))
    wqn = (jnp.concatenate(qn, axis=-1) * scale).astype(BF16)
    wra = (jnp.concatenate(ra, axis=-1) * scale).astype(BF16)
    wrb = (jnp.concatenate(rb, axis=-1) * scale).astype(BF16)
    wukt = jnp.stack(ukt, axis=1).astype(BF16)
    wuvp = jnp.stack(uvp, axis=1).astype(BF16)
    inv = ROPE_THETA ** (-jnp.arange(half, dtype=F32) / half)
    pad = jnp.zeros((LANES - MLA_ROPE,), F32)
    freq = jnp.concatenate([inv, inv, pad])[None, :]
    sign = jnp.concatenate([-jnp.ones((half,), F32), jnp.ones((half,), F32), pad])[None, :]
    return wqn, wukt, wra, wrb, wuvp, freq, sign


def _causal_mask(t):
    row = lax.broadcasted_iota(jnp.int32, (t, t), 0)
    col = lax.broadcasted_iota(jnp.int32, (t, t), 1)
    return row >= col


def _lane_fold(x, op):
    r = x[:, :LANES]
    for c in range(1, x.shape[1] // LANES):
        r = op(r, x[:, c * LANES:(c + 1) * LANES])
    return r


class _Softmax:
    def __init__(self, q, m_ref, l_ref, acc_ref):
        self.q, self.m_ref, self.l_ref, self.acc_ref = q, m_ref, l_ref, acc_ref
        m_ref[...] = jnp.full(m_ref.shape, NEG, F32)
        l_ref[...] = jnp.zeros(l_ref.shape, F32)
        acc_ref[...] = jnp.zeros(acc_ref.shape, F32)

    def tile(self, k, v, mask_fn=None):
        s = lax.dot_general(self.q, k, _NT, preferred_element_type=F32)
        if mask_fn is not None:
            s = mask_fn(s)
        m_old = self.m_ref[...]
        m_new = jnp.maximum(m_old, jnp.max(_lane_fold(s, jnp.maximum), axis=-1, keepdims=True))
        alpha = jnp.exp2(m_old - m_new)
        parts = [jnp.exp2(s[:, c * LANES:(c + 1) * LANES] - m_new) for c in range(s.shape[1] // LANES)]
        lsum = parts[0]
        for p in parts[1:]:
            lsum = lsum + p
        self.l_ref[...] = alpha * self.l_ref[...] + lsum
        p = jnp.concatenate(parts, axis=1).astype(BF16)
        self.acc_ref[...] = alpha * self.acc_ref[...] + _dot(p, v)
        self.m_ref[...] = m_new

    def result(self):
        return self.acc_ref[...] / jnp.sum(self.l_ref[...], axis=-1, keepdims=True)


def _local_rc(nh, tq):
    row = lax.broadcasted_iota(jnp.int32, (nh * tq, tq), 0) & (tq - 1)
    col = lax.broadcasted_iota(jnp.int32, (nh * tq, tq), 1)
    return row, col


def _mask_last_block(s, keep, tq):
    w = s.shape[1]
    last = jnp.where(keep, s[:, w - tq:], NEG)
    return last if w == tq else jnp.concatenate([s[:, :w - tq], last], axis=1)


def _causal_sweep(sm, i, nh, tq, tk, load):
    n_full = (i * tq) // tk

    def body(j, c):
        sm.tile(*load(pl.multiple_of(j * tk, tk), tk))
        return c

    lax.fori_loop(0, n_full, body, 0)
    row, col = _local_rc(nh, tq)
    per = tk // tq
    for r in range(per):
        @pl.when(i % per == r)
        def _():
            k, v = load(pl.multiple_of(n_full * tk, tk), (r + 1) * tq)
            sm.tile(k, v, lambda s: _mask_last_block(s, col <= row, tq))


def _mla_attn_kernel(q_ref, k_ref, wuvp_ref, o_ref, qs_ref, m_ref, l_ref, acc_ref, *, tq, tk):
    i = pl.program_id(1)
    for h in range(MLA_HEADS):
        qs_ref[h * tq:(h + 1) * tq, :] = q_ref[:, h * _MLA_QW:(h + 1) * _MLA_QW]
    sm = _Softmax(qs_ref[...], m_ref, l_ref, acc_ref)

    def load(start, width):
        k = k_ref[pl.ds(start, width), :]
        return k, k[:, :LANES]

    _causal_sweep(sm, i, MLA_HEADS, tq, tk, load)
    o_lat = sm.result().astype(BF16)
    out = _dot(o_lat[:tq], wuvp_ref[0])
    for h in range(1, MLA_HEADS):
        out = out + _dot(o_lat[h * tq:(h + 1) * tq], wuvp_ref[h])
    o_ref[...] = out.astype(BF16)


_ATT_TQ, _ATT_TK = 256, 1024


def mla_attn(q, k, wuvp, B, S):
    tq, tk = min(_ATT_TQ, S), min(_ATT_TK, S)
    assert tq & (tq - 1) == 0
    M = MLA_HEADS * tq
    q3, k3 = q.reshape(B, S, MLA_HEADS * _MLA_QW), k.reshape(B, S, _MLA_QW)
    out = pl.pallas_call(
        functools.partial(_mla_attn_kernel, tq=tq, tk=tk), grid=(B, S // tq),
        in_specs=[pl.BlockSpec((None, tq, MLA_HEADS * _MLA_QW), lambda b, i: (b, i, 0)),
                  pl.BlockSpec((None, S, _MLA_QW), lambda b, i: (b, 0, 0)),
                  pl.BlockSpec(wuvp.shape, lambda b, i: (0, 0, 0))],
        out_specs=pl.BlockSpec((None, tq, 256), lambda b, i: (b, i, 0)),
        out_shape=jax.ShapeDtypeStruct((B, S, 256), BF16),
        scratch_shapes=[pltpu.VMEM((M, _MLA_QW), BF16), pltpu.VMEM((M, LANES), F32),
                        pltpu.VMEM((M, LANES), F32), pltpu.VMEM((M, LANES), F32)],
        compiler_params=_cparams(("parallel", "arbitrary")), name="mla_attn")(q3, k3, wuvp)
    return out.reshape(B * S, 256)


def _conv_gmlp_kernel(xc_ref, halo_ref, xg_ref, cw_ref, ln_ref, ws_ref, bs_ref, o_ref, *, tiles_per_seq):
    i = pl.program_id(0)
    tm = xc_ref.shape[0]
    xc = xc_ref[...].astype(F32)
    cb, u = xc[:, :256], xc[:, 256:512] * xc[:, 512:768]
    hx = halo_ref[...].astype(F32)
    keep = jnp.where(i % tiles_per_seq == 0, 0.0, 1.0)
    hu = hx[:, 256:512] * hx[:, 512:768] * keep
    p1, p2 = hu[7:8, :], hu[6:7, :]
    row = lax.broadcasted_iota(jnp.int32, (tm, 256), 0)
    u1 = jnp.where(row == 0, p1, pltpu.roll(u, 1, axis=0))
    u2 = jnp.where(row == 0, p2, jnp.where(row == 1, p1, pltpu.roll(u, 2, axis=0)))
    cw = cw_ref[...]
    y_conv = cb * (cw[0:1, :] * u2 + cw[1:2, :] * u1 + cw[2:3, :] * u)
    o_ref[:, :256] = y_conv.astype(BF16)

    xg = xg_ref[...].astype(F32)
    gu = _gelu(xg[:, :256])
    gv = _gelu(xg[:, 256:512])
    mu = jnp.mean(gv, axis=-1, keepdims=True)
    var = jnp.mean(jnp.square(gv - mu), axis=-1, keepdims=True)
    vn = (gv - mu) * lax.rsqrt(var + EPS) * ln_ref[...]
    lane_g = lax.broadcasted_iota(jnp.int32, (GMLP_CHUNK, 256), 1) // (256 // GMLP_GROUPS)
    tril = _causal_mask(GMLP_CHUNK)
    for c in range(tm // GMLP_CHUNK):
        vc = vn[c * GMLP_CHUNK:(c + 1) * GMLP_CHUNK, :]
        sv = bs_ref[...]
        for g in range(GMLP_GROUPS):
            wg = jnp.where(tril, ws_ref[g], 0.0).astype(BF16)
            sv = sv + _dot(wg, jnp.where(lane_g == g, vc, 0.0).astype(BF16))
        o_ref[c * GMLP_CHUNK:(c + 1) * GMLP_CHUNK, 256:512] = (
            gu[c * GMLP_CHUNK:(c + 1) * GMLP_CHUNK, :] * sv).astype(BF16)


def conv_gmlp(x_conv, x_gmlp, conv_w8, ln_g, w_s, b_exp, S):
    T = x_conv.shape[0]
    tm = min(512, S)
    full = lambda a: pl.BlockSpec(a.shape, lambda i: (0,) * a.ndim)
    return pl.pallas_call(
        functools.partial(_conv_gmlp_kernel, tiles_per_seq=S // tm), grid=(T // tm,),
        in_specs=[pl.BlockSpec((tm, _W_CONV), lambda i: (i, 0)),
                  pl.BlockSpec((8, _W_CONV), lambda i: (jnp.maximum(i * (tm // 8) - 1, 0), 0)),
                  pl.BlockSpec((tm, _W_GMLP), lambda i: (i, 0)),
                  full(conv_w8), full(ln_g), full(w_s), full(b_exp)],
        out_specs=pl.BlockSpec((tm, 512), lambda i: (i, 0)),
        out_shape=jax.ShapeDtypeStruct((T, 512), BF16),
        compiler_params=_cparams(("parallel",)), name="conv_gmlp")(
            x_conv, x_conv, x_gmlp, conv_w8, ln_g, w_s, b_exp)


def _compress_kernel(z_ref, pea_ref, peb_ref, w1a_ref, w1b_ref, w2_ref, o_ref):
    z = z_ref[...]
    n = z.shape[0]
    a = _dot(z, w1a_ref[...])
    b = _dot(z, w1b_ref[...])
    c = _dot_exact(pea_ref[...], w1a_ref[...].astype(F32)) + _dot_exact(peb_ref[...], w1b_ref[...].astype(F32))
    pre = a + pltpu.roll(b, n - 1, axis=0) + c[0:1, :]
    o_ref[...] = _dot(_gelu(pre).astype(BF16), w2_ref[...]).astype(BF16)


def compress(z2, pea, peb, w1a, w1b, w2):
    B, n, K = z2.shape
    full = lambda a: pl.BlockSpec(a.shape, lambda b: (0,) * a.ndim)
    return pl.pallas_call(
        _compress_kernel, grid=(B,),
        in_specs=[pl.BlockSpec((None, n, K), lambda b: (b, 0, 0)),
                  full(pea), full(peb), full(w1a), full(w1b), full(w2)],
        out_specs=pl.BlockSpec((None, n, LANES), lambda b: (b, 0, 0)),
        out_shape=jax.ShapeDtypeStruct((B, n, LANES), BF16),
        compiler_params=_cparams(("parallel",)), name="nsa_compress")(z2, pea, peb, w1a, w1b, w2)


def prep_compress(pe_k, w1_k, w2_k, pe_v, w1_v, w2_v):
    half = CMP_LEN // 2

    def w1_half(w1k, w1v):
        wk = w1k.reshape(half, NSA_DH, CMP_HIDDEN)
        wv = w1v.reshape(half, NSA_DH, CMP_HIDDEN)
        zk = jnp.zeros_like(wk)
        top = jnp.concatenate([wk, zk], axis=-1)
        bot = jnp.concatenate([zk, wv], axis=-1)
        return jnp.concatenate([top, bot], axis=1).reshape(half * 2 * NSA_DH, 2 * CMP_HIDDEN)

    n1 = half * NSA_DH
    w1a = w1_half(w1_k[:n1], w1_v[:n1]).astype(BF16)
    w1b = w1_half(w1_k[n1:], w1_v[n1:]).astype(BF16)

    def pe_half(pk, pv):
        row = jnp.concatenate([pk, pv], axis=-1).reshape(1, half * 2 * NSA_DH)
        return jnp.broadcast_to(row, (8, half * 2 * NSA_DH))

    pea = pe_half(pe_k[:half], pe_v[:half])
    peb = pe_half(pe_k[half:], pe_v[half:])
    z = jnp.zeros((CMP_HIDDEN, NSA_DH), F32)
    w2 = jnp.concatenate([jnp.concatenate([w2_k, z], axis=1),
                          jnp.concatenate([z, w2_v], axis=1)], axis=0).astype(BF16)
    return pea, peb, w1a, w1b, w2


def _head_select(nh):
    sel = np.zeros((nh, nh * NSA_DH, LANES), np.float32)
    place = np.zeros((nh, LANES, nh * NSA_DH), np.float32)
    for h in range(nh):
        for d in range(NSA_DH):
            sel[h, h * NSA_DH + d, d] = 1.0
            place[h, NSA_DH + d, h * NSA_DH + d] = 1.0
    return jnp.asarray(sel, BF16), jnp.asarray(place, BF16)


def _stack_heads(q, sel_ref, qs_ref):
    for h in range(NSA_HEADS):
        qs_ref[h] = _dot(q, sel_ref[h]).astype(BF16)


def _place_heads(o_list, place_ref):
    out = _dot(o_list[0].astype(BF16), place_ref[0])
    for h in range(1, NSA_HEADS):
        out = out + _dot(o_list[h].astype(BF16), place_ref[h])
    return out


def _gate(gl, h, branch):
    c = h * 3 + branch
    return 1.0 / (1.0 + jnp.exp(-gl[:, c:c + 1]))


def _nsa_cmp_kernel(q_ref, kv_ref, gl_ref, msel_ref, sel_ref, place_ref, o_ref, bias_ref, qs_ref, *, t, ns):
    i = pl.program_id(1)
    nc = kv_ref.shape[0]
    _stack_heads(q_ref[...], sel_ref, qs_ref)
    kv = kv_ref[...]
    gl = gl_ref[...].astype(F32)
    tpos = i * t + lax.broadcasted_iota(jnp.int32, (t, nc), 0)
    blk_end = lax.broadcasted_iota(jnp.int32, (t, nc), 1) * CMP_STRIDE + (CMP_LEN - 1)
    cmask = blk_end <= tpos
    p_sum = jnp.zeros((t, nc), F32)
    outs = []
    for h in range(NSA_HEADS):
        s = lax.dot_general(qs_ref[h], kv, _NT, preferred_element_type=F32)
        s = jnp.where(cmask, s, NEG)
        m = jnp.max(s, axis=-1, keepdims=True)
        e = jnp.where(cmask, jnp.exp2(s - m), 0.0)
        l = jnp.sum(e, axis=-1, keepdims=True)
        p = e / jnp.where(l > 0.0, l, 1.0)
        p_sum = p_sum + p
        outs.append(_gate(gl, h, 0) * _dot(p.astype(BF16), kv))
    o_ref[...] = _place_heads(outs, place_ref).astype(BF16)

    p_sel = _dot_exact(p_sum, msel_ref[...])
    blk = lax.broadcasted_iota(jnp.int32, (t, LANES), 1) - NSA_DH
    cur = (i * t + lax.broadcasted_iota(jnp.int32, (t, LANES), 0)) // SEL_LEN
    forced = (blk == 0) | (blk == cur) | (blk == cur - 1)
    v = jnp.where(forced, FORCE_SCORE, jnp.where(blk <= cur, p_sel, -1.0))
    v = jnp.where((blk >= 0) & (blk < ns), v, -jnp.inf)
    vt = v.T[NSA_DH:, :]
    bt = lax.broadcasted_iota(jnp.int32, vt.shape, 0).astype(F32)
    chosen = jnp.zeros(vt.shape, jnp.bool_)
    for _ in range(min(N_SELECT, ns)):
        mx = jnp.max(vt, axis=0, keepdims=True)
        idx = jnp.min(jnp.where(vt == mx, bt, float(LANES)), axis=0, keepdims=True)
        hit = bt == idx
        chosen = chosen | hit
        vt = jnp.where(hit, -jnp.inf, vt)
    bias_t = jnp.concatenate([jnp.zeros(vt.shape, F32), jnp.where(chosen, 0.0, NEG)], axis=0)
    bias_ref[...] = bias_t.T.astype(BF16)


def nsa_cmp(q, kcv, gl, msel, sel, place, B, S):
    t = min(1024, S)
    nc = kcv.shape[1]
    ns = S // SEL_LEN
    full = lambda a: pl.BlockSpec(a.shape, lambda b, i: (0,) * a.ndim)
    return pl.pallas_call(
        functools.partial(_nsa_cmp_kernel, t=t, ns=ns), grid=(B, S // t),
        in_specs=[pl.BlockSpec((None, t, 256), lambda b, i: (b, i, 0)),
                  pl.BlockSpec((None, nc, LANES), lambda b, i: (b, 0, 0)),
                  pl.BlockSpec((None, t, LANES), lambda b, i: (b, i, 0)),
                  full(msel), full(sel), full(place)],
        out_specs=[pl.BlockSpec((None, t, 256), lambda b, i: (b, i, 0)),
                   pl.BlockSpec((None, t, LANES), lambda b, i: (b, i, 0))],
        out_shape=[jax.ShapeDtypeStruct((B, S, 256), BF16), jax.ShapeDtypeStruct((B, S, LANES), BF16)],
        scratch_shapes=[pltpu.VMEM((NSA_HEADS, t, LANES), BF16)],
        compiler_params=_cparams(("parallel", "arbitrary")), name="nsa_cmp")(
            q.reshape(B, S, 256), kcv, gl.reshape(B, S, LANES), msel, sel, place)


def cmp_to_sel(S):
    nc = S // CMP_STRIDE - CMP_LEN // CMP_STRIDE + 1
    ns = S // SEL_LEN
    cs = np.arange(nc)[:, None] * CMP_STRIDE
    ss = np.arange(ns)[None, :] * SEL_LEN
    ov = np.clip(np.minimum(cs + CMP_LEN, ss + SEL_LEN) - np.maximum(cs, ss), 0, None)
    assert ns <= LANES - NSA_DH, "selection blocks must fit the lanes beside one head's query"
    m = np.zeros((S // CMP_STRIDE, LANES), np.float32)
    m[:nc, NSA_DH:NSA_DH + ns] = ov.astype(np.float32) / np.float32(CMP_LEN)
    return jnp.asarray(m)


def _nsa_attn_kernel(q_ref, ksel_ref, kwin_ref, bias_ref, ocmp_ref, gl_ref, sel_ref, place_ref,
                     o_ref, kx_ref, kwp_ref, qs_ref, qw_ref, m_ref, l_ref, acc_ref, *, tq, tk):
    i = pl.program_id(1)
    S = ksel_ref.shape[0]

    @pl.when(i == 0)
    def _():
        r = lax.broadcasted_iota(jnp.int32, (S, LANES), 0)
        c = lax.broadcasted_iota(jnp.int32, (S, LANES), 1)
        onehot = jnp.where(r // SEL_LEN == c - NSA_DH, 1.0, 0.0)
        kx_ref[...] = jnp.where(c < NSA_DH, ksel_ref[...].astype(F32), onehot).astype(BF16)
        kwp_ref[:WINDOW, :] = jnp.zeros((WINDOW, LANES), BF16)
        kwp_ref[WINDOW:, :] = kwin_ref[...]

    q = q_ref[...]
    bias = bias_ref[...].astype(F32)
    for h in range(NSA_HEADS):
        qs = _dot(q, sel_ref[h])
        qw_ref[h * tq:(h + 1) * tq, :] = qs.astype(BF16)
        qs_ref[h * tq:(h + 1) * tq, :] = (qs + bias).astype(BF16)

    def heads(x):
        return [x[h * tq:(h + 1) * tq] for h in range(NSA_HEADS)]

    sm = _Softmax(qs_ref[...], m_ref, l_ref, acc_ref)
    _causal_sweep(sm, i, NSA_HEADS, tq, tk,
                  lambda start, width: (kx_ref[pl.ds(start, width), :], ksel_ref[pl.ds(start, width), :]))
    o_sel = heads(sm.result())

    sm = _Softmax(qw_ref[...], m_ref, l_ref, acc_ref)
    row, col = _local_rc(NSA_HEADS, tq)
    nblk = WINDOW // tq + 1

    def win_mask(s):
        blocks = []
        for c in range(nblk):
            sc = s[:, c * tq:(c + 1) * tq]
            if c == 0:
                sc = jnp.where(row < col, sc, NEG)
            if c == nblk - 1:
                sc = jnp.where(col <= row, sc, NEG)
            else:
                sc = sc + jnp.where(i >= nblk - 1 - c, 0.0, NEG)
            blocks.append(sc)
        return jnp.concatenate(blocks, axis=1)

    kw = kwp_ref[pl.ds(pl.multiple_of(i * tq, tq), WINDOW + tq), :]
    sm.tile(kw, kw, win_mask)
    o_win = heads(sm.result())

    gl = gl_ref[...].astype(F32)
    gated = [_gate(gl, h, 1) * o_sel[h] + _gate(gl, h, 2) * o_win[h] for h in range(NSA_HEADS)]
    o_ref[...] = (ocmp_ref[...].astype(F32) + _place_heads(gated, place_ref)).astype(BF16)


def nsa_attn(q, ksel, kwin, bias, o_cmp, gl, sel, place, B, S):
    tq, tk = min(_ATT_TQ, S), min(_ATT_TK, S)
    assert WINDOW % tq == 0
    M = NSA_HEADS * tq
    full = lambda a: pl.BlockSpec(a.shape, lambda b, i: (0,) * a.ndim)
    tile2 = lambda w: pl.BlockSpec((None, tq, w), lambda b, i: (b, i, 0))
    seq = pl.BlockSpec((None, S, LANES), lambda b, i: (b, 0, 0))
    out = pl.pallas_call(
        functools.partial(_nsa_attn_kernel, tq=tq, tk=tk), grid=(B, S // tq),
        in_specs=[tile2(256), seq, seq, tile2(LANES), tile2(256), tile2(LANES), full(sel), full(place)],
        out_specs=tile2(256),
        out_shape=jax.ShapeDtypeStruct((B, S, 256), BF16),
        scratch_shapes=[pltpu.VMEM((S, LANES), BF16), pltpu.VMEM((S + WINDOW, LANES), BF16),
                        pltpu.VMEM((M, LANES), BF16), pltpu.VMEM((M, LANES), BF16),
                        pltpu.VMEM((M, LANES), F32), pltpu.VMEM((M, LANES), F32), pltpu.VMEM((M, LANES), F32)],
        compiler_params=_cparams(("parallel", "arbitrary")), name="nsa_attn")(
            q.reshape(B, S, 256), ksel.reshape(B, S, LANES), kwin.reshape(B, S, LANES), bias, o_cmp,
            gl.reshape(B, S, LANES), sel, place)
    return out.reshape(B * S, 256)


def _mix_out_kernel(ymla_ref, ycg_ref, ynsa_ref, h_ref, gn_ref, wo_ref, o_ref):
    gn = gn_ref[...]
    parts = [ymla_ref[...].astype(F32), ycg_ref[:, :256].astype(F32), ycg_ref[:, 256:].astype(F32),
             ynsa_ref[...].astype(F32)]
    yn = [_rms(p, gn[:, k * GROUP_W:(k + 1) * GROUP_W]).astype(BF16) for k, p in enumerate(parts)]
    o_ref[...] = h_ref[...] + _dot(jnp.concatenate(yn, axis=1), wo_ref[...])


def mix_out(y_mla, y_cg, y_nsa, h, gn, wo):
    T, D = h.shape
    tm = min(512, T)
    row = lambda w: pl.BlockSpec((tm, w), lambda i: (i, 0))
    return pl.pallas_call(
        _mix_out_kernel, grid=(T // tm,),
        in_specs=[row(256), row(512), row(256), row(D),
                  pl.BlockSpec((1, D), lambda i: (0, 0)), pl.BlockSpec((D, D), lambda i: (0, 0))],
        out_specs=row(D), out_shape=jax.ShapeDtypeStruct((T, D), F32),
        compiler_params=_cparams(("parallel",)), name="mix_out")(y_mla, y_cg, y_nsa, h, gn, wo)


def _ffn_step(j, nj, x_ref, g_ref, wg_ref, wu_ref, wd_ref, o_ref, hn_ref, acc_ref, residual):
    @pl.when(j == 0)
    def _():
        hn_ref[...] = _rms(x_ref[...], g_ref[...]).astype(BF16)
        acc_ref[...] = jnp.zeros(acc_ref.shape, F32)

    hn = hn_ref[...]
    a = _dot(hn, wg_ref[...])
    b = _dot(hn, wu_ref[...])
    act = (a / (1.0 + jnp.exp(-a)) * b).astype(BF16)
    acc_ref[...] += _dot(act, wd_ref[...])

    @pl.when(j == nj - 1)
    def _():
        o_ref[...] = (x_ref[...] + acc_ref[...]) if residual else acc_ref[...]


def _ffn_kernel(x_ref, g_ref, wg_ref, wu_ref, wd_ref, o_ref, hn_ref, acc_ref):
    _ffn_step(pl.program_id(1), pl.num_programs(1), x_ref, g_ref, wg_ref, wu_ref, wd_ref, o_ref, hn_ref,
              acc_ref, True)


_FFN_TF = 512


def ffn_dense(h, g, wg, wu, wd, layer):
    T, D = h.shape
    F = wg.shape[2]
    tm = min(1024, T)
    return pl.pallas_call(
        _ffn_kernel, grid=(T // tm, F // _FFN_TF),
        in_specs=[pl.BlockSpec((tm, D), lambda i, j: (i, 0)),
                  pl.BlockSpec((1, D), lambda i, j: (0, 0)),
                  pl.BlockSpec((None, D, _FFN_TF), lambda i, j: (layer, 0, j)),
                  pl.BlockSpec((None, D, _FFN_TF), lambda i, j: (layer, 0, j)),
                  pl.BlockSpec((None, _FFN_TF, D), lambda i, j: (layer, j, 0))],
        out_specs=pl.BlockSpec((tm, D), lambda i, j: (i, 0)),
        out_shape=jax.ShapeDtypeStruct((T, D), F32),
        scratch_shapes=[pltpu.VMEM((tm, D), BF16), pltpu.VMEM((tm, D), F32)],
        compiler_params=_cparams(("parallel", "arbitrary")), name="ffn_dense")(h, g, wg, wu, wd)


def _ffn_grouped_kernel(te_ref, nu_ref, x_ref, g_ref, wg_ref, wu_ref, wd_ref, o_ref, hn_ref, acc_ref):
    del te_ref
    i, j, nj = pl.program_id(0), pl.program_id(1), pl.num_programs(1)
    used = i < nu_ref[0]

    @pl.when(used)
    def _():
        _ffn_step(j, nj, x_ref, g_ref, wg_ref, wu_ref, wd_ref, o_ref, hn_ref, acc_ref, False)

    @pl.when(jnp.logical_not(used))
    def _():
        o_ref[...] = jnp.zeros(o_ref.shape, F32)


def ffn_grouped(xs, g, wg, wu, wd, layer, tile_expert, n_used, tm):
    NP, D = xs.shape
    F = wg.shape[3]
    nf = F // _FFN_TF

    def jj(i, j, nu):
        return jnp.where(i < nu[0], j, nf - 1)

    grid_spec = pltpu.PrefetchScalarGridSpec(
        num_scalar_prefetch=2, grid=(NP // tm, nf),
        in_specs=[pl.BlockSpec((tm, D), lambda i, j, te, nu: (i, 0)),
                  pl.BlockSpec((1, D), lambda i, j, te, nu: (0, 0)),
                  pl.BlockSpec((None, None, D, _FFN_TF), lambda i, j, te, nu: (layer, te[i], 0, jj(i, j, nu))),
                  pl.BlockSpec((None, None, D, _FFN_TF), lambda i, j, te, nu: (layer, te[i], 0, jj(i, j, nu))),
                  pl.BlockSpec((None, None, _FFN_TF, D), lambda i, j, te, nu: (layer, te[i], jj(i, j, nu), 0))],
        out_specs=pl.BlockSpec((tm, D), lambda i, j, te, nu: (i, 0)),
        scratch_shapes=[pltpu.VMEM((tm, D), BF16), pltpu.VMEM((tm, D), F32)])
    return pl.pallas_call(
        _ffn_grouped_kernel, grid_spec=grid_spec,
        out_shape=jax.ShapeDtypeStruct((NP, D), F32),
        compiler_params=_cparams(("arbitrary", "arbitrary")), name="ffn_grouped")(
            tile_expert, n_used, xs, g, wg, wu, wd)


def _router_kernel(h_ref, g_ref, wr_ref, o_ref):
    hn = _rms(h_ref[...], g_ref[...])
    logits = _dot_exact(hn, wr_ref[...])
    tm = logits.shape[0]
    lane = lax.broadcasted_iota(jnp.int32, (tm, LANES), 1)
    lane_f = lane.astype(F32)
    lg = jnp.where(lane < N_EXPERTS, logits, -jnp.inf)
    m1 = jnp.max(lg, axis=-1, keepdims=True)
    i1 = jnp.min(jnp.where(lg == m1, lane_f, float(LANES)), axis=-1, keepdims=True)
    lg2 = jnp.where(lane_f == i1, -jnp.inf, lg)
    m2 = jnp.max(lg2, axis=-1, keepdims=True)
    i2 = jnp.min(jnp.where(lg2 == m2, lane_f, float(LANES)), axis=-1, keepdims=True)
    e = jnp.exp(m2 - m1)
    w1 = 1.0 / (1.0 + e)
    w2 = e / (1.0 + e)
    o_ref[...] = jnp.where(lane == 0, i1,
                           jnp.where(lane == 1, i2,
                                     jnp.where(lane == 2, w1, jnp.where(lane == 3, w2, 0.0))))


def router(h, g, wr):
    T, D = h.shape
    tm = min(512, T)
    return pl.pallas_call(
        _router_kernel, grid=(T // tm,),
        in_specs=[pl.BlockSpec((tm, D), lambda i: (i, 0)), pl.BlockSpec((1, D), lambda i: (0, 0)),
                  pl.BlockSpec((D, LANES), lambda i: (0, 0))],
        out_specs=pl.BlockSpec((tm, LANES), lambda i: (i, 0)),
        out_shape=jax.ShapeDtypeStruct((T, LANES), F32),
        compiler_params=_cparams(("parallel",)), name="moe_router")(h, g, wr)


def _row_copy(src, s, dst, d, sem):
    return pltpu.make_async_copy(src.at[pl.ds(s, 1)], dst.at[pl.ds(d, 1)], sem)


def _dispatch_kernel(dest_ref, h_ref, xs_in_ref, xs_ref, sem):
    del xs_in_ref
    tm = h_ref.shape[0]

    def issue(r, c):
        _row_copy(h_ref, r, xs_ref, dest_ref[0, 0, 2 * r], sem).start()
        _row_copy(h_ref, r, xs_ref, dest_ref[0, 0, 2 * r + 1], sem).start()
        return c

    lax.fori_loop(0, tm, issue, 0, unroll=8)
    for _ in range(2):
        pltpu.make_async_copy(h_ref, xs_ref.at[pl.ds(0, tm)], sem).wait()


def dispatch(h, dest, xs_zero):
    T, D = h.shape
    tm = min(512, T)
    dest3 = dest.reshape(T // tm, 1, 2 * tm)
    return pl.pallas_call(
        _dispatch_kernel, grid=(T // tm,),
        in_specs=[pl.BlockSpec((1, 1, 2 * tm), lambda i: (i, 0, 0), memory_space=pltpu.SMEM),
                  pl.BlockSpec((tm, D), lambda i: (i, 0)),
                  pl.BlockSpec(memory_space=pl.ANY)],
        out_specs=pl.BlockSpec(memory_space=pl.ANY),
        out_shape=jax.ShapeDtypeStruct(xs_zero.shape, F32),
        scratch_shapes=[pltpu.SemaphoreType.DMA(())],
        input_output_aliases={2: 0},
        compiler_params=_cparams(("arbitrary",)), name="moe_dispatch")(dest3, h, xs_zero)


def _combine_kernel(dest_ref, dnext_ref, h_ref, route_ref, *rest, out_norm):
    og_ref = rest[0] if out_norm else None
    ys_ref, o_ref, b0_ref, b1_ref, sem = rest[1:] if out_norm else rest
    i, n = pl.program_id(0), pl.num_programs(0)
    tm = h_ref.shape[0]
    slot = i % 2

    def gather(d_ref, s):
        def issue(r, c):
            _row_copy(ys_ref, d_ref[0, 0, 2 * r], b0_ref.at[s], r, sem.at[s]).start()
            _row_copy(ys_ref, d_ref[0, 0, 2 * r + 1], b1_ref.at[s], r, sem.at[s]).start()
            return c

        lax.fori_loop(0, tm, issue, 0, unroll=8)

    @pl.when(i == 0)
    def _():
        gather(dest_ref, 0)

    @pl.when(i + 1 < n)
    def _():
        gather(dnext_ref, 1 - slot)

    pltpu.make_async_copy(ys_ref.at[pl.ds(0, tm)], b0_ref.at[slot], sem.at[slot]).wait()
    pltpu.make_async_copy(ys_ref.at[pl.ds(0, tm)], b1_ref.at[slot], sem.at[slot]).wait()
    route = route_ref[...]
    out = h_ref[...] + route[:, 2:3] * b0_ref[slot] + route[:, 3:4] * b1_ref[slot]
    o_ref[...] = _rms(out, og_ref[...]) if out_norm else out


def combine(h, route, dest, ys, out_g=None):
    T, D = h.shape
    tm = min(256, T)
    n = T // tm
    dest3 = dest.reshape(n, 1, 2 * tm)
    smem = lambda f: pl.BlockSpec((1, 1, 2 * tm), f, memory_space=pltpu.SMEM)
    out_norm = out_g is not None
    norm_spec = [pl.BlockSpec((1, D), lambda i: (0, 0))] if out_norm else []
    norm_arg = [out_g] if out_norm else []
    return pl.pallas_call(
        functools.partial(_combine_kernel, out_norm=out_norm), grid=(n,),
        in_specs=[smem(lambda i: (i, 0, 0)), smem(lambda i: (jnp.minimum(i + 1, n - 1), 0, 0)),
                  pl.BlockSpec((tm, D), lambda i: (i, 0)),
                  pl.BlockSpec((tm, LANES), lambda i: (i, 0))] + norm_spec +
                 [pl.BlockSpec(memory_space=pl.ANY)],
        out_specs=pl.BlockSpec((tm, D), lambda i: (i, 0)),
        out_shape=jax.ShapeDtypeStruct((T, D), F32),
        scratch_shapes=[pltpu.VMEM((2, tm, D), F32), pltpu.VMEM((2, tm, D), F32),
                        pltpu.SemaphoreType.DMA((2,))],
        compiler_params=_cparams(("arbitrary",)), name="moe_combine")(dest3, dest3, h, route, *norm_arg, ys)


def moe_ffn(h, g, wr, wg, wu, wd, layer, xs_buf=None, out_g=None):
    T, D = h.shape
    tm = min(1024, T)
    route = router(h, g, wr)
    e_flat = route[:, :2].astype(jnp.int32).reshape(2 * T)
    onehot = (e_flat[:, None] == jnp.arange(N_EXPERTS, dtype=jnp.int32)[None, :]).astype(jnp.int32)
    csum = jnp.cumsum(onehot, axis=0)
    counts = csum[-1]
    rank = jnp.sum((csum - 1) * onehot, axis=1)
    tiles = (counts + tm - 1) // tm
    tile_end = jnp.cumsum(tiles)
    offs = (tile_end - tiles) * tm
    dest = (jnp.sum(onehot * offs[None, :], axis=1) + rank).astype(jnp.int32)
    n_tiles = (2 * T) // tm + N_EXPERTS
    n_used = tile_end[-1].astype(jnp.int32)
    tile_ids = jnp.minimum(jnp.arange(n_tiles, dtype=jnp.int32), n_used - 1)
    tile_expert = jnp.sum((tile_ids[:, None] >= tile_end[None, :]).astype(jnp.int32), axis=1).astype(jnp.int32)
    if xs_buf is None:
        xs_buf = jnp.zeros((n_tiles * tm, D), F32)
    xs = dispatch(h, dest, xs_buf)
    ys = ffn_grouped(xs, g, wg, wu, wd, layer, tile_expert, n_used.reshape(1), tm)
    return combine(h, route, dest, ys, out_g), xs


def _final_norm_kernel(h_ref, g_ref, o_ref):
    o_ref[...] = _rms(h_ref[...], g_ref[...])


def final_rms(h, g):
    T, D = h.shape
    tm = min(1024, T)
    return pl.pallas_call(
        _final_norm_kernel, grid=(T // tm,),
        in_specs=[pl.BlockSpec((tm, D), lambda i: (i, 0)), pl.BlockSpec((1, D), lambda i: (0, 0))],
        out_specs=pl.BlockSpec((tm, D), lambda i: (i, 0)),
        out_shape=jax.ShapeDtypeStruct((T, D), F32),
        compiler_params=_cparams(("parallel",)), name="final_norm")(h, g)


def kernel(x, positions, attn_norm, w_in, mla_q_norm, mla_w_uq, mla_kv_norm, mla_w_ukv, conv_w, gmlp_ln, gmlp_w_s, gmlp_b_s, nsa_pe_k, nsa_w1_k, nsa_w2_k, nsa_pe_v, nsa_w1_v, nsa_w2_v, group_norm, w_o, ffn_norm, dense_w_gate, dense_w_up, dense_w_down, moe_router, moe_w_gate, moe_w_up, moe_w_down, final_norm):
    B, S, D = x.shape
    T = B * S
    depth = w_in.shape[0]
    h = x.reshape(T, D)
    pos = positions.astype(F32).reshape(T, 1)

    w_in_p = prep_w_in(w_in)
    wqn, wukt, wra, wrb, wuvp, freq, sign = prep_mla(mla_w_uq, mla_w_ukv)
    rope_cos, rope_sin = rope_tables(pos, freq, sign)
    conv_w8 = jnp.pad(conv_w, ((0, 0), (0, 8 - CONV_K), (0, 0)))
    b_exp = jnp.repeat(jnp.swapaxes(gmlp_b_s, 1, 2), GROUP_W // GMLP_GROUPS, axis=2)
    sel, place = _head_select(NSA_HEADS)
    msel = cmp_to_sel(S)
    w_o_b = w_o.astype(BF16)
    wr_p = jnp.pad(moe_router, ((0, 0), (0, 0), (0, LANES - N_EXPERTS)))
    moe_wg, moe_wu, moe_wd = moe_w_gate.astype(BF16), moe_w_up.astype(BF16), moe_w_down.astype(BF16)
    dense_wg, dense_wu, dense_wd = (dense_w_gate.astype(BF16), dense_w_up.astype(BF16),
                                    dense_w_down.astype(BF16))
    xs_buf = None

    for l in range(depth):
        x_mla, x_conv, x_gmlp, x_nq, x_cmp, x_sel, x_win, x_gate = in_proj(h, attn_norm[l][None, :], w_in_p[l])

        q, k = mla_proj(x_mla, rope_cos, rope_sin, mla_q_norm[l][None, :], mla_kv_norm[l][None, :],
                        wqn[l], wukt[l], wra[l], wrb[l])
        y_mla = mla_attn(q, k, wuvp[l], B, S)

        y_cg = conv_gmlp(x_conv, x_gmlp, conv_w8[l], gmlp_ln[l][None, :], gmlp_w_s[l], b_exp[l], S)

        pea, peb, w1a, w1b, w2 = prep_compress(nsa_pe_k[l], nsa_w1_k[l], nsa_w2_k[l],
                                               nsa_pe_v[l], nsa_w1_v[l], nsa_w2_v[l])
        z2 = x_cmp.reshape(B, S // CMP_STRIDE, CMP_STRIDE * LANES)
        kcv = compress(z2, pea, peb, w1a, w1b, w2)
        o_cmp, sel_bias = nsa_cmp(x_nq, kcv, x_gate, msel, sel, place, B, S)
        y_nsa = nsa_attn(x_nq, x_sel, x_win, sel_bias, o_cmp, x_gate, sel, place, B, S)

        h = mix_out(y_mla, y_cg, y_nsa, h, group_norm[l].reshape(1, D), w_o_b[l])

        fg = ffn_norm[l][None, :]
        if l % 2 == 0:
            h = ffn_dense(h, fg, dense_wg, dense_wu, dense_wd, l // 2)
        else:
            out_g = final_norm[None, :] if l == depth - 1 else None
            h, xs_buf = moe_ffn(h, fg, wr_p[l // 2], moe_wg, moe_wu, moe_wd, l // 2, xs_buf, out_g)
    if depth % 2 == 1:
        h = final_rms(h, final_norm[None, :])
    return h.reshape(B, S, D)
```

```python
import functools

import numpy as np
import jax
import jax.numpy as jnp
from jax import lax
from jax.experimental import pallas as pl
from jax.experimental.pallas import tpu as pltpu

F32 = jnp.float32
BF16 = jnp.bfloat16

D_MODEL = 1024
GROUP_W = 256
MLA_HEADS = 4
MLA_NOPE = 64
MLA_ROPE = 32
MLA_V = 64
MLA_Q_RANK = 256
MLA_KV_RANK = 128
ROPE_THETA = 10000.0
CONV_K = 3
GMLP_GROUPS = 4
GMLP_CHUNK = 128
NSA_HEADS = 4
NSA_DH = 64
CMP_LEN = 32
CMP_STRIDE = 16
CMP_HIDDEN = 256
SEL_LEN = 64
N_SELECT = 16
WINDOW = 512
FORCE_SCORE = 1e4
D_FF = 3584
N_EXPERTS = 8
EPS = 1e-6
NEG = -1e30
LOG2E = 1.4426950408889634

LANES = 128
VMEM_LIMIT = 56 * 1024 * 1024

_IN_SIZES = (256, 128, 32, 256, 256, 256, 256, 256, 256, 64, 64, 64, 64, 64, 64, 12)
_IN_OFF = np.concatenate([[0], np.cumsum(_IN_SIZES)]).tolist()

_W_MLA, _W_CONV, _W_GMLP, _W_NQ, _W_KV, _W_GATE = 640, 768, 512, 256, 128, 128
_IN_SPLITS = (_W_MLA, _W_CONV, _W_GMLP, _W_NQ, _W_KV, _W_KV, _W_KV, _W_GATE)
_D_IN_P = sum(_IN_SPLITS)

_NT = (((1,), (1,)), ((), ()))


def _cparams(sem, vmem=VMEM_LIMIT):
    return pltpu.CompilerParams(dimension_semantics=sem, vmem_limit_bytes=vmem)


def _rms(x, g):
    return x * lax.rsqrt(jnp.mean(x * x, axis=-1, keepdims=True) + EPS) * g


def _dot(a, b):
    return jnp.dot(a, b, preferred_element_type=F32)


def _dot_exact(a, b):
    return jnp.dot(a, b, preferred_element_type=F32, precision=lax.Precision.HIGHEST)


def _gelu(x):
    return 0.5 * x * (1.0 + jnp.tanh(0.7978845608028654 * (x + 0.044715 * (x * x * x))))


def _in_proj_kernel(h_ref, g_ref, w_ref, *o_refs):
    xn = _rms(h_ref[...], g_ref[...]).astype(BF16)
    r = _dot(xn, w_ref[...]).astype(BF16)
    off = 0
    for o_ref, w in zip(o_refs, _IN_SPLITS):
        o_ref[...] = r[:, off:off + w]
        off += w


def in_proj(h, g, w):
    T, D = h.shape
    tm = min(512, T)
    return pl.pallas_call(
        _in_proj_kernel, grid=(T // tm,),
        in_specs=[pl.BlockSpec((tm, D), lambda i: (i, 0)),
                  pl.BlockSpec((1, D), lambda i: (0, 0)),
                  pl.BlockSpec((D, _D_IN_P), lambda i: (0, 0))],
        out_specs=[pl.BlockSpec((tm, w_), lambda i: (i, 0)) for w_ in _IN_SPLITS],
        out_shape=[jax.ShapeDtypeStruct((T, w_), BF16) for w_ in _IN_SPLITS],
        compiler_params=_cparams(("parallel",)), name="in_proj")(h, g, w)


def prep_w_in(w_in):
    o = _IN_OFF
    L, D, _ = w_in.shape
    z = lambda n: jnp.zeros((L, D, n), F32)
    kr = w_in[:, :, o[2]:o[3]]
    half = MLA_ROPE // 2
    kr_sw = jnp.concatenate([kr[..., half:], kr[..., :half]], axis=-1)
    cols = [w_in[:, :, o[0]:o[2]],
            kr, z(96), kr_sw, z(96),
            w_in[:, :, o[3]:o[6]],
            w_in[:, :, o[6]:o[8]],
            w_in[:, :, o[8]:o[9]] * (NSA_DH ** -0.5 * LOG2E),
            w_in[:, :, o[9]:o[15]],
            w_in[:, :, o[15]:o[16]], z(_W_GATE - 12)]
    return jnp.concatenate(cols, axis=-1).astype(BF16)


_MLA_QW = 2 * LANES


def _rope_tables_kernel(pos_ref, freq_ref, sign_ref, c_ref, s_ref):
    ang = pos_ref[...] * freq_ref[...]
    c_ref[...] = jnp.cos(ang)
    s_ref[...] = jnp.sin(ang) * sign_ref[...]


def rope_tables(pos, freq, sign):
    T = pos.shape[0]
    tm = min(1024, T)
    row = pl.BlockSpec((tm, LANES), lambda i: (i, 0))
    one = pl.BlockSpec((1, LANES), lambda i: (0, 0))
    return pl.pallas_call(
        _rope_tables_kernel, grid=(T // tm,),
        in_specs=[pl.BlockSpec((tm, 1), lambda i: (i, 0)), one, one],
        out_specs=[row, row],
        out_shape=[jax.ShapeDtypeStruct((T, LANES), F32), jax.ShapeDtypeStruct((T, LANES), F32)],
        compiler_params=_cparams(("parallel",)), name="rope_tables")(pos, freq, sign)


def _mla_proj_kernel(x_ref, c_ref, s_ref, qg_ref, kvg_ref, wqn_ref, wukt_ref, wra_ref, wrb_ref, q_ref, k_ref):
    x = x_ref[...].astype(F32)
    cq, ckv = x[:, :256], x[:, 256:384]
    kra, krb = x[:, 384:512], x[:, 512:640]
    c, s = c_ref[...], s_ref[...]
    cqn = _rms(cq, qg_ref[...]).astype(BF16)
    qn = _dot(cqn, wqn_ref[...])
    ra = _dot(cqn, wra_ref[...])
    rb = _dot(cqn, wrb_ref[...])
    for h in range(MLA_HEADS):
        hs = slice(h * LANES, (h + 1) * LANES)
        q_ref[:, h * _MLA_QW:h * _MLA_QW + LANES] = _dot(qn[:, hs].astype(BF16), wukt_ref[h]).astype(BF16)
        q_ref[:, h * _MLA_QW + LANES:(h + 1) * _MLA_QW] = (ra[:, hs] * c + rb[:, hs] * s).astype(BF16)
    k_ref[:, :LANES] = _rms(ckv, kvg_ref[...]).astype(BF16)
    k_ref[:, LANES:] = (kra * c + krb * s).astype(BF16)


def mla_proj(x_mla, cos, sin, qg, kvg, wqn, wukt, wra, wrb):
    T = x_mla.shape[0]
    tm = min(512, T)
    full = lambda a: pl.BlockSpec(a.shape, lambda i: (0,) * a.ndim)
    return pl.pallas_call(
        _mla_proj_kernel, grid=(T // tm,),
        in_specs=[pl.BlockSpec((tm, _W_MLA), lambda i: (i, 0)),
                  pl.BlockSpec((tm, LANES), lambda i: (i, 0)),
                  pl.BlockSpec((tm, LANES), lambda i: (i, 0)),
                  full(qg), full(kvg), full(wqn), full(wukt), full(wra), full(wrb)],
        out_specs=[pl.BlockSpec((tm, MLA_HEADS * _MLA_QW), lambda i: (i, 0)),
                   pl.BlockSpec((tm, _MLA_QW), lambda i: (i, 0))],
        out_shape=[jax.ShapeDtypeStruct((T, MLA_HEADS * _MLA_QW), BF16),
                   jax.ShapeDtypeStruct((T, _MLA_QW), BF16)],
        compiler_params=_cparams(("parallel",)), name="mla_proj")(
            x_mla, cos, sin, qg, kvg, wqn, wukt, wra, wrb)


def prep_mla(w_uq, w_ukv):
    L = w_uq.shape[0]
    scale = (MLA_NOPE + MLA_ROPE) ** -0.5 * LOG2E
    half = MLA_ROPE // 2
    hq = MLA_NOPE + MLA_ROPE
    hk = MLA_NOPE + MLA_V
    zq = lambda n: jnp.zeros((L, MLA_Q_RANK, n), F32)
    qn, ra, rb, ukt, uvp = [], [], [], [], []
    for h in range(MLA_HEADS):
        rp = w_uq[:, :, h * hq + MLA_NOPE:(h + 1) * hq]
        qn += [w_uq[:, :, h * hq:h * hq + MLA_NOPE], zq(LANES - MLA_NOPE)]
        ra += [rp, zq(LANES - MLA_ROPE)]
        rb += [jnp.concatenate([rp[..., half:], rp[..., :half]], axis=-1), zq(LANES - MLA_ROPE)]
        w_uk = w_ukv[:, :, h * hk:h * hk + MLA_NOPE]
        ukt.append(jnp.pad(jnp.swapaxes(w_uk, 1, 2), ((0, 0), (0, LANES - MLA_NOPE), (0, 0))))
        w_uv = w_ukv[:, :, h * hk + MLA_NOPE:(h + 1) * hk]
        uvp.append(jnp.pad(w_uv, ((0, 0), (0, 0), (h * MLA_V, (MLA_HEADS - 1 - h) * MLA_V))))
    wqn = (jnp.concatenate(qn, axis=-1) * scale).astype(BF16)
    wra = (jnp.concatenate(ra, axis=-1) * scale).astype(BF16)
    wrb = (jnp.concatenate(rb, axis=-1) * scale).astype(BF16)
    wukt = jnp.stack(ukt, axis=1).astype(BF16)
    wuvp = jnp.stack(uvp, axis=1).astype(BF16)
    inv = ROPE_THETA ** (-jnp.arange(half, dtype=F32) / half)
    pad = jnp.zeros((LANES - MLA_ROPE,), F32)
    freq = jnp.concatenate([inv, inv, pad])[None, :]
    sign = jnp.concatenate([-jnp.ones((half,), F32), jnp.ones((half,), F32), pad])[None, :]
    return wqn, wukt, wra, wrb, wuvp, freq, sign


def _causal_mask(t):
    row = lax.broadcasted_iota(jnp.int32, (t, t), 0)
    col = lax.broadcasted_iota(jnp.int32, (t, t), 1)
    return row >= col


def _lane_fold(x, op):
    r = x[:, :LANES]
    for c in range(1, x.shape[1] // LANES):
        r = op(r, x[:, c * LANES:(c + 1) * LANES])
    return r


class _Softmax:
    def __init__(self, q, m_ref, l_ref, acc_ref):
        self.q, self.m_ref, self.l_ref, self.acc_ref = q, m_ref, l_ref, acc_ref
        m_ref[...] = jnp.full(m_ref.shape, NEG, F32)
        l_ref[...] = jnp.zeros(l_ref.shape, F32)
        acc_ref[...] = jnp.zeros(acc_ref.shape, F32)

    def tile(self, k, v, mask_fn=None):
        s = lax.dot_general(self.q, k, _NT, preferred_element_type=F32)
        if mask_fn is not None:
            s = mask_fn(s)
        m_old = self.m_ref[...]
        m_new = jnp.maximum(m_old, jnp.max(_lane_fold(s, jnp.maximum), axis=-1, keepdims=True))
        alpha = jnp.exp2(m_old - m_new)
        parts = [jnp.exp2(s[:, c * LANES:(c + 1) * LANES] - m_new) for c in range(s.shape[1] // LANES)]
        lsum = parts[0]
        for p in parts[1:]:
            lsum = lsum + p
        self.l_ref[...] = alpha * self.l_ref[...] + lsum
        p = jnp.concatenate(parts, axis=1).astype(BF16)
        self.acc_ref[...] = alpha * self.acc_ref[...] + _dot(p, v)
        self.m_ref[...] = m_new

    def result(self):
        return self.acc_ref[...] / jnp.sum(self.l_ref[...], axis=-1, keepdims=True)


def _local_rc(nh, tq):
    row = lax.broadcasted_iota(jnp.int32, (nh * tq, tq), 0) & (tq - 1)
    col = lax.broadcasted_iota(jnp.int32, (nh * tq, tq), 1)
    return row, col


def _mask_last_block(s, keep, tq):
    w = s.shape[1]
    last = jnp.where(keep, s[:, w - tq:], NEG)
    return last if w == tq else jnp.concatenate([s[:, :w - tq], last], axis=1)


def _causal_sweep(sm, i, nh, tq, tk, load):
    n_full = (i * tq) // tk

    def body(j, c):
        sm.tile(*load(pl.multiple_of(j * tk, tk), tk))
        return c

    lax.fori_loop(0, n_full, body, 0)
    row, col = _local_rc(nh, tq)
    per = tk // tq
    for r in range(per):
        @pl.when(i % per == r)
        def _():
            k, v = load(pl.multiple_of(n_full * tk, tk), (r + 1) * tq)
            sm.tile(k, v, lambda s: _mask_last_block(s, col <= row, tq))


def _mla_attn_kernel(q_ref, k_ref, wuvp_ref, o_ref, qs_ref, m_ref, l_ref, acc_ref, *, tq, tk):
    i = pl.program_id(1)
    for h in range(MLA_HEADS):
        qs_ref[h * tq:(h + 1) * tq, :] = q_ref[:, h * _MLA_QW:(h + 1) * _MLA_QW]
    sm = _Softmax(qs_ref[...], m_ref, l_ref, acc_ref)

    def load(start, width):
        k = k_ref[pl.ds(start, width), :]
        return k, k[:, :LANES]

    _causal_sweep(sm, i, MLA_HEADS, tq, tk, load)
    o_lat = sm.result().astype(BF16)
    out = _dot(o_lat[:tq], wuvp_ref[0])
    for h in range(1, MLA_HEADS):
        out = out + _dot(o_lat[h * tq:(h + 1) * tq], wuvp_ref[h])
    o_ref[...] = out.astype(BF16)


_ATT_TQ, _ATT_TK = 512, 1024


def mla_attn(q, k, wuvp, B, S):
    tq, tk = min(_ATT_TQ, S), min(_ATT_TK, S)
    assert tq & (tq - 1) == 0
    M = MLA_HEADS * tq
    q3, k3 = q.reshape(B, S, MLA_HEADS * _MLA_QW), k.reshape(B, S, _MLA_QW)
    out = pl.pallas_call(
        functools.partial(_mla_attn_kernel, tq=tq, tk=tk), grid=(B, S // tq),
        in_specs=[pl.BlockSpec((None, tq, MLA_HEADS * _MLA_QW), lambda b, i: (b, i, 0)),
                  pl.BlockSpec((None, S, _MLA_QW), lambda b, i: (b, 0, 0)),
                  pl.BlockSpec(wuvp.shape, lambda b, i: (0, 0, 0))],
        out_specs=pl.BlockSpec((None, tq, 256), lambda b, i: (b, i, 0)),
        out_shape=jax.ShapeDtypeStruct((B, S, 256), BF16),
        scratch_shapes=[pltpu.VMEM((M, _MLA_QW), BF16), pltpu.VMEM((M, LANES), F32),
                        pltpu.VMEM((M, LANES), F32), pltpu.VMEM((M, LANES), F32)],
        compiler_params=_cparams(("parallel", "arbitrary")), name="mla_attn")(q3, k3, wuvp)
    return out.reshape(B * S, 256)


def _conv_gmlp_kernel(xc_ref, halo_ref, xg_ref, cw_ref, ln_ref, ws_ref, bs_ref, o_ref, *, tiles_per_seq):
    i = pl.program_id(0)
    tm = xc_ref.shape[0]
    xc = xc_ref[...].astype(F32)
    cb, u = xc[:, :256], xc[:, 256:512] * xc[:, 512:768]
    hx = halo_ref[...].astype(F32)
    keep = jnp.where(i % tiles_per_seq == 0, 0.0, 1.0)
    hu = hx[:, 256:512] * hx[:, 512:768] * keep
    p1, p2 = hu[7:8, :], hu[6:7, :]
    row = lax.broadcasted_iota(jnp.int32, (tm, 256), 0)
    u1 = jnp.where(row == 0, p1, pltpu.roll(u, 1, axis=0))
    u2 = jnp.where(row == 0, p2, jnp.where(row == 1, p1, pltpu.roll(u, 2, axis=0)))
    cw = cw_ref[...]
    y_conv = cb * (cw[0:1, :] * u2 + cw[1:2, :] * u1 + cw[2:3, :] * u)
    o_ref[:, :256] = y_conv.astype(BF16)

    xg = xg_ref[...].astype(F32)
    gu = _gelu(xg[:, :256])
    gv = _gelu(xg[:, 256:512])
    mu = jnp.mean(gv, axis=-1, keepdims=True)
    var = jnp.mean(jnp.square(gv - mu), axis=-1, keepdims=True)
    vn = (gv - mu) * lax.rsqrt(var + EPS) * ln_ref[...]
    lane_g = lax.broadcasted_iota(jnp.int32, (GMLP_CHUNK, 256), 1) // (256 // GMLP_GROUPS)
    tril = _causal_mask(GMLP_CHUNK)
    for c in range(tm // GMLP_CHUNK):
        vc = vn[c * GMLP_CHUNK:(c + 1) * GMLP_CHUNK, :]
        sv = bs_ref[...]
        for g in range(GMLP_GROUPS):
            wg = jnp.where(tril, ws_ref[g], 0.0).astype(BF16)
            sv = sv + _dot(wg, jnp.where(lane_g == g, vc, 0.0).astype(BF16))
        o_ref[c * GMLP_CHUNK:(c + 1) * GMLP_CHUNK, 256:512] = (
            gu[c * GMLP_CHUNK:(c + 1) * GMLP_CHUNK, :] * sv).astype(BF16)


def conv_gmlp(x_conv, x_gmlp, conv_w8, ln_g, w_s, b_exp, S):
    T = x_conv.shape[0]
    tm = min(512, S)
    full = lambda a: pl.BlockSpec(a.shape, lambda i: (0,) * a.ndim)
    return pl.pallas_call(
        functools.partial(_conv_gmlp_kernel, tiles_per_seq=S // tm), grid=(T // tm,),
        in_specs=[pl.BlockSpec((tm, _W_CONV), lambda i: (i, 0)),
                  pl.BlockSpec((8, _W_CONV), lambda i: (jnp.maximum(i * (tm // 8) - 1, 0), 0)),
                  pl.BlockSpec((tm, _W_GMLP), lambda i: (i, 0)),
                  full(conv_w8), full(ln_g), full(w_s), full(b_exp)],
        out_specs=pl.BlockSpec((tm, 512), lambda i: (i, 0)),
        out_shape=jax.ShapeDtypeStruct((T, 512), BF16),
        compiler_params=_cparams(("parallel",)), name="conv_gmlp")(
            x_conv, x_conv, x_gmlp, conv_w8, ln_g, w_s, b_exp)


def _compress_kernel(z_ref, pea_ref, peb_ref, w1a_ref, w1b_ref, w2_ref, o_ref):
    z = z_ref[...]
    n = z.shape[0]
    a = _dot(z, w1a_ref[...])
    b = _dot(z, w1b_ref[...])
    c = _dot_exact(pea_ref[...], w1a_ref[...].astype(F32)) + _dot_exact(peb_ref[...], w1b_ref[...].astype(F32))
    pre = a + pltpu.roll(b, n - 1, axis=0) + c[0:1, :]
    o_ref[...] = _dot(_gelu(pre).astype(BF16), w2_ref[...]).astype(BF16)


def compress(z2, pea, peb, w1a, w1b, w2):
    B, n, K = z2.shape
    full = lambda a: pl.BlockSpec(a.shape, lambda b: (0,) * a.ndim)
    return pl.pallas_call(
        _compress_kernel, grid=(B,),
        in_specs=[pl.BlockSpec((None, n, K), lambda b: (b, 0, 0)),
                  full(pea), full(peb), full(w1a), full(w1b), full(w2)],
        out_specs=pl.BlockSpec((None, n, LANES), lambda b: (b, 0, 0)),
        out_shape=jax.ShapeDtypeStruct((B, n, LANES), BF16),
        compiler_params=_cparams(("parallel",)), name="nsa_compress")(z2, pea, peb, w1a, w1b, w2)


def prep_compress(pe_k, w1_k, w2_k, pe_v, w1_v, w2_v):
    half = CMP_LEN // 2

    def w1_half(w1k, w1v):
        wk = w1k.reshape(half, NSA_DH, CMP_HIDDEN)
        wv = w1v.reshape(half, NSA_DH, CMP_HIDDEN)
        zk = jnp.zeros_like(wk)
        top = jnp.concatenate([wk, zk], axis=-1)
        bot = jnp.concatenate([zk, wv], axis=-1)
        return jnp.concatenate([top, bot], axis=1).reshape(half * 2 * NSA_DH, 2 * CMP_HIDDEN)

    n1 = half * NSA_DH
    w1a = w1_half(w1_k[:n1], w1_v[:n1]).astype(BF16)
    w1b = w1_half(w1_k[n1:], w1_v[n1:]).astype(BF16)

    def pe_half(pk, pv):
        row = jnp.concatenate([pk, pv], axis=-1).reshape(1, half * 2 * NSA_DH)
        return jnp.broadcast_to(row, (8, half * 2 * NSA_DH))

    pea = pe_half(pe_k[:half], pe_v[:half])
    peb = pe_half(pe_k[half:], pe_v[half:])
    z = jnp.zeros((CMP_HIDDEN, NSA_DH), F32)
    w2 = jnp.concatenate([jnp.concatenate([w2_k, z], axis=1),
                          jnp.concatenate([z, w2_v], axis=1)], axis=0).astype(BF16)
    return pea, peb, w1a, w1b, w2


def _head_select(nh):
    sel = np.zeros((nh, nh * NSA_DH, LANES), np.float32)
    place = np.zeros((nh, LANES, nh * NSA_DH), np.float32)
    for h in range(nh):
        for d in range(NSA_DH):
            sel[h, h * NSA_DH + d, d] = 1.0
            place[h, NSA_DH + d, h * NSA_DH + d] = 1.0
    return jnp.asarray(sel, BF16), jnp.asarray(place, BF16)


def _stack_heads(q, sel_ref, qs_ref):
    for h in range(NSA_HEADS):
        qs_ref[h] = _dot(q, sel_ref[h]).astype(BF16)


def _place_heads(o_list, place_ref):
    out = _dot(o_list[0].astype(BF16), place_ref[0])
    for h in range(1, NSA_HEADS):
        out = out + _dot(o_list[h].astype(BF16), place_ref[h])
    return out


def _gate(gl, h, branch):
    c = h * 3 + branch
    return 1.0 / (1.0 + jnp.exp(-gl[:, c:c + 1]))


def _nsa_cmp_kernel(q_ref, kv_ref, gl_ref, msel_ref, sel_ref, place_ref, o_ref, bias_ref, qs_ref, *, t, ns):
    i = pl.program_id(1)
    nc = kv_ref.shape[0]
    _stack_heads(q_ref[...], sel_ref, qs_ref)
    kv = kv_ref[...]
    gl = gl_ref[...].astype(F32)
    tpos = i * t + lax.broadcasted_iota(jnp.int32, (t, nc), 0)
    blk_end = lax.broadcasted_iota(jnp.int32, (t, nc), 1) * CMP_STRIDE + (CMP_LEN - 1)
    cmask = blk_end <= tpos
    p_sum = jnp.zeros((t, nc), F32)
    outs = []
    for h in range(NSA_HEADS):
        s = lax.dot_general(qs_ref[h], kv, _NT, preferred_element_type=F32)
        s = jnp.where(cmask, s, NEG)
        m = jnp.max(s, axis=-1, keepdims=True)
        e = jnp.where(cmask, jnp.exp2(s - m), 0.0)
        l = jnp.sum(e, axis=-1, keepdims=True)
        p = e / jnp.where(l > 0.0, l, 1.0)
        p_sum = p_sum + p
        outs.append(_gate(gl, h, 0) * _dot(p.astype(BF16), kv))
    o_ref[...] = _place_heads(outs, place_ref).astype(BF16)

    p_sel = _dot_exact(p_sum, msel_ref[...])
    blk = lax.broadcasted_iota(jnp.int32, (t, LANES), 1) - NSA_DH
    cur = (i * t + lax.broadcasted_iota(jnp.int32, (t, LANES), 0)) // SEL_LEN
    forced = (blk == 0) | (blk == cur) | (blk == cur - 1)
    v = jnp.where(forced, FORCE_SCORE, jnp.where(blk <= cur, p_sel, -1.0))
    v = jnp.where((blk >= 0) & (blk < ns), v, -jnp.inf)
    vt = v.T[NSA_DH:, :]
    bt = lax.broadcasted_iota(jnp.int32, vt.shape, 0).astype(F32)
    chosen = jnp.zeros(vt.shape, jnp.bool_)
    for _ in range(min(N_SELECT, ns)):
        mx = jnp.max(vt, axis=0, keepdims=True)
        idx = jnp.min(jnp.where(vt == mx, bt, float(LANES)), axis=0, keepdims=True)
        hit = bt == idx
        chosen = chosen | hit
        vt = jnp.where(hit, -jnp.inf, vt)
    bias_t = jnp.concatenate([jnp.zeros(vt.shape, F32), jnp.where(chosen, 0.0, NEG)], axis=0)
    bias_ref[...] = bias_t.T.astype(BF16)


def nsa_cmp(q, kcv, gl, msel, sel, place, B, S):
    t = min(1024, S)
    nc = kcv.shape[1]
    ns = S // SEL_LEN
    full = lambda a: pl.BlockSpec(a.shape, lambda b, i: (0,) * a.ndim)
    return pl.pallas_call(
        functools.partial(_nsa_cmp_kernel, t=t, ns=ns), grid=(B, S // t),
        in_specs=[pl.BlockSpec((None, t, 256), lambda b, i: (b, i, 0)),
                  pl.BlockSpec((None, nc, LANES), lambda b, i: (b, 0, 0)),
                  pl.BlockSpec((None, t, LANES), lambda b, i: (b, i, 0)),
                  full(msel), full(sel), full(place)],
        out_specs=[pl.BlockSpec((None, t, 256), lambda b, i: (b, i, 0)),
                   pl.BlockSpec((None, t, LANES), lambda b, i: (b, i, 0))],
        out_shape=[jax.ShapeDtypeStruct((B, S, 256), BF16), jax.ShapeDtypeStruct((B, S, LANES), BF16)],
        scratch_shapes=[pltpu.VMEM((NSA_HEADS, t, LANES), BF16)],
        compiler_params=_cparams(("parallel", "arbitrary")), name="nsa_cmp")(
            q.reshape(B, S, 256), kcv, gl.reshape(B, S, LANES), msel, sel, place)


def cmp_to_sel(S):
    nc = S // CMP_STRIDE - CMP_LEN // CMP_STRIDE + 1
    ns = S // SEL_LEN
    cs = np.arange(nc)[:, None] * CMP_STRIDE
    ss = np.arange(ns)[None, :] * SEL_LEN
    ov = np.clip(np.minimum(cs + CMP_LEN, ss + SEL_LEN) - np.maximum(cs, ss), 0, None)
    assert ns <= LANES - NSA_DH, "selection blocks must fit the lanes beside one head's query"
    m = np.zeros((S // CMP_STRIDE, LANES), np.float32)
    m[:nc, NSA_DH:NSA_DH + ns] = ov.astype(np.float32) / np.float32(CMP_LEN)
    return jnp.asarray(m)


def _nsa_attn_kernel(q_ref, ksel_ref, kwin_ref, bias_ref, ocmp_ref, gl_ref, sel_ref, place_ref,
                     o_ref, kx_ref, kwp_ref, qs_ref, qw_ref, m_ref, l_ref, acc_ref, *, tq, tk):
    i = pl.program_id(1)
    S = ksel_ref.shape[0]

    @pl.when(i == 0)
    def _():
        r = lax.broadcasted_iota(jnp.int32, (S, LANES), 0)
        c = lax.broadcasted_iota(jnp.int32, (S, LANES), 1)
        onehot = jnp.where(r // SEL_LEN == c - NSA_DH, 1.0, 0.0)
        kx_ref[...] = jnp.where(c < NSA_DH, ksel_ref[...].astype(F32), onehot).astype(BF16)
        kwp_ref[:WINDOW, :] = jnp.zeros((WINDOW, LANES), BF16)
        kwp_ref[WINDOW:, :] = kwin_ref[...]

    q = q_ref[...]
    bias = bias_ref[...].astype(F32)
    for h in range(NSA_HEADS):
        qs = _dot(q, sel_ref[h])
        qw_ref[h * tq:(h + 1) * tq, :] = qs.astype(BF16)
        qs_ref[h * tq:(h + 1) * tq, :] = (qs + bias).astype(BF16)

    def heads(x):
        return [x[h * tq:(h + 1) * tq] for h in range(NSA_HEADS)]

    sm = _Softmax(qs_ref[...], m_ref, l_ref, acc_ref)
    _causal_sweep(sm, i, NSA_HEADS, tq, tk,
                  lambda start, width: (kx_ref[pl.ds(start, width), :], ksel_ref[pl.ds(start, width), :]))
    o_sel = heads(sm.result())

    sm = _Softmax(qw_ref[...], m_ref, l_ref, acc_ref)
    row, col = _local_rc(NSA_HEADS, tq)
    nblk = WINDOW // tq + 1

    def win_mask(s):
        blocks = []
        for c in range(nblk):
            sc = s[:, c * tq:(c + 1) * tq]
            if c == 0:
                sc = jnp.where(row < col, sc, NEG)
            if c == nblk - 1:
                sc = jnp.where(col <= row, sc, NEG)
            else:
                sc = sc + jnp.where(i >= nblk - 1 - c, 0.0, NEG)
            blocks.append(sc)
        return jnp.concatenate(blocks, axis=1)

    kw = kwp_ref[pl.ds(pl.multiple_of(i * tq, tq), WINDOW + tq), :]
    sm.tile(kw, kw, win_mask)
    o_win = heads(sm.result())

    gl = gl_ref[...].astype(F32)
    gated = [_gate(gl, h, 1) * o_sel[h] + _gate(gl, h, 2) * o_win[h] for h in range(NSA_HEADS)]
    o_ref[...] = (ocmp_ref[...].astype(F32) + _place_heads(gated, place_ref)).astype(BF16)


def nsa_attn(q, ksel, kwin, bias, o_cmp, gl, sel, place, B, S):
    tq, tk = min(_ATT_TQ, S), min(_ATT_TK, S)
    assert WINDOW % tq == 0
    M = NSA_HEADS * tq
    full = lambda a: pl.BlockSpec(a.shape, lambda b, i: (0,) * a.ndim)
    tile2 = lambda w: pl.BlockSpec((None, tq, w), lambda b, i: (b, i, 0))
    seq = pl.BlockSpec((None, S, LANES), lambda b, i: (b, 0, 0))
    out = pl.pallas_call(
        functools.partial(_nsa_attn_kernel, tq=tq, tk=tk), grid=(B, S // tq),
        in_specs=[tile2(256), seq, seq, tile2(LANES), tile2(256), tile2(LANES), full(sel), full(place)],
        out_specs=tile2(256),
        out_shape=jax.ShapeDtypeStruct((B, S, 256), BF16),
        scratch_shapes=[pltpu.VMEM((S, LANES), BF16), pltpu.VMEM((S + WINDOW, LANES), BF16),
                        pltpu.VMEM((M, LANES), BF16), pltpu.VMEM((M, LANES), BF16),
                        pltpu.VMEM((M, LANES), F32), pltpu.VMEM((M, LANES), F32), pltpu.VMEM((M, LANES), F32)],
        compiler_params=_cparams(("parallel", "arbitrary")), name="nsa_attn")(
            q.reshape(B, S, 256), ksel.reshape(B, S, LANES), kwin.reshape(B, S, LANES), bias, o_cmp,
            gl.reshape(B, S, LANES), sel, place)
    return out.reshape(B * S, 256)


def _mix_out_kernel(ymla_ref, ycg_ref, ynsa_ref, h_ref, gn_ref, wo_ref, o_ref):
    gn = gn_ref[...]
    parts = [ymla_ref[...].astype(F32), ycg_ref[:, :256].astype(F32), ycg_ref[:, 256:].astype(F32),
             ynsa_ref[...].astype(F32)]
    yn = [_rms(p, gn[:, k * GROUP_W:(k + 1) * GROUP_W]).astype(BF16) for k, p in enumerate(parts)]
    o_ref[...] = h_ref[...] + _dot(jnp.concatenate(yn, axis=1), wo_ref[...])


def mix_out(y_mla, y_cg, y_nsa, h, gn, wo):
    T, D = h.shape
    tm = min(512, T)
    row = lambda w: pl.BlockSpec((tm, w), lambda i: (i, 0))
    return pl.pallas_call(
        _mix_out_kernel, grid=(T // tm,),
        in_specs=[row(256), row(512), row(256), row(D),
                  pl.BlockSpec((1, D), lambda i: (0, 0)), pl.BlockSpec((D, D), lambda i: (0, 0))],
        out_specs=row(D), out_shape=jax.ShapeDtypeStruct((T, D), F32),
        compiler_params=_cparams(("parallel",)), name="mix_out")(y_mla, y_cg, y_nsa, h, gn, wo)


def _ffn_step(j, nj, x_ref, g_ref, wg_ref, wu_ref, wd_ref, o_ref, hn_ref, acc_ref, residual):
    @pl.when(j == 0)
    def _():
        hn_ref[...] = _rms(x_ref[...], g_ref[...]).astype(BF16)
        acc_ref[...] = jnp.zeros(acc_ref.shape, F32)

    hn = hn_ref[...]
    a = _dot(hn, wg_ref[...])
    b = _dot(hn, wu_ref[...])
    act = (a / (1.0 + jnp.exp(-a)) * b).astype(BF16)
    acc_ref[...] += _dot(act, wd_ref[...])

    @pl.when(j == nj - 1)
    def _():
        o_ref[...] = (x_ref[...] + acc_ref[...]) if residual else acc_ref[...]


def _ffn_kernel(x_ref, g_ref, wg_ref, wu_ref, wd_ref, o_ref, hn_ref, acc_ref):
    _ffn_step(pl.program_id(1), pl.num_programs(1), x_ref, g_ref, wg_ref, wu_ref, wd_ref, o_ref, hn_ref,
              acc_ref, True)


_FFN_TF = 512


def ffn_dense(h, g, wg, wu, wd, layer):
    T, D = h.shape
    F = wg.shape[2]
    tm = min(1024, T)
    return pl.pallas_call(
        _ffn_kernel, grid=(T // tm, F // _FFN_TF),
        in_specs=[pl.BlockSpec((tm, D), lambda i, j: (i, 0)),
                  pl.BlockSpec((1, D), lambda i, j: (0, 0)),
                  pl.BlockSpec((None, D, _FFN_TF), lambda i, j: (layer, 0, j)),
                  pl.BlockSpec((None, D, _FFN_TF), lambda i, j: (layer, 0, j)),
                  pl.BlockSpec((None, _FFN_TF, D), lambda i, j: (layer, j, 0))],
        out_specs=pl.BlockSpec((tm, D), lambda i, j: (i, 0)),
        out_shape=jax.ShapeDtypeStruct((T, D), F32),
        scratch_shapes=[pltpu.VMEM((tm, D), BF16), pltpu.VMEM((tm, D), F32)],
        compiler_params=_cparams(("parallel", "arbitrary")), name="ffn_dense")(h, g, wg, wu, wd)


def _ffn_grouped_kernel(te_ref, nu_ref, x_ref, g_ref, wg_ref, wu_ref, wd_ref, o_ref, hn_ref, acc_ref):
    del te_ref
    i, j, nj = pl.program_id(0), pl.program_id(1), pl.num_programs(1)
    used = i < nu_ref[0]

    @pl.when(used)
    def _():
        _ffn_step(j, nj, x_ref, g_ref, wg_ref, wu_ref, wd_ref, o_ref, hn_ref, acc_ref, False)

    @pl.when(jnp.logical_not(used))
    def _():
        o_ref[...] = jnp.zeros(o_ref.shape, F32)


def ffn_grouped(xs, g, wg, wu, wd, layer, tile_expert, n_used, tm):
    NP, D = xs.shape
    F = wg.shape[3]
    nf = F // _FFN_TF

    def jj(i, j, nu):
        return jnp.where(i < nu[0], j, nf - 1)

    grid_spec = pltpu.PrefetchScalarGridSpec(
        num_scalar_prefetch=2, grid=(NP // tm, nf),
        in_specs=[pl.BlockSpec((tm, D), lambda i, j, te, nu: (i, 0)),
                  pl.BlockSpec((1, D), lambda i, j, te, nu: (0, 0)),
                  pl.BlockSpec((None, None, D, _FFN_TF), lambda i, j, te, nu: (layer, te[i], 0, jj(i, j, nu))),
                  pl.BlockSpec((None, None, D, _FFN_TF), lambda i, j, te, nu: (layer, te[i], 0, jj(i, j, nu))),
                  pl.BlockSpec((None, None, _FFN_TF, D), lambda i, j, te, nu: (layer, te[i], jj(i, j, nu), 0))],
        out_specs=pl.BlockSpec((tm, D), lambda i, j, te, nu: (i, 0)),
        scratch_shapes=[pltpu.VMEM((tm, D), BF16), pltpu.VMEM((tm, D), F32)])
    return pl.pallas_call(
        _ffn_grouped_kernel, grid_spec=grid_spec,
        out_shape=jax.ShapeDtypeStruct((NP, D), F32),
        compiler_params=_cparams(("arbitrary", "arbitrary")), name="ffn_grouped")(
            tile_expert, n_used, xs, g, wg, wu, wd)


def _router_kernel(h_ref, g_ref, wr_ref, o_ref):
    hn = _rms(h_ref[...], g_ref[...])
    logits = _dot_exact(hn, wr_ref[...])
    tm = logits.shape[0]
    lane = lax.broadcasted_iota(jnp.int32, (tm, LANES), 1)
    lane_f = lane.astype(F32)
    lg = jnp.where(lane < N_EXPERTS, logits, -jnp.inf)
    m1 = jnp.max(lg, axis=-1, keepdims=True)
    i1 = jnp.min(jnp.where(lg == m1, lane_f, float(LANES)), axis=-1, keepdims=True)
    lg2 = jnp.where(lane_f == i1, -jnp.inf, lg)
    m2 = jnp.max(lg2, axis=-1, keepdims=True)
    i2 = jnp.min(jnp.where(lg2 == m2, lane_f, float(LANES)), axis=-1, keepdims=True)
    e = jnp.exp(m2 - m1)
    w1 = 1.0 / (1.0 + e)
    w2 = e / (1.0 + e)
    o_ref[...] = jnp.where(lane == 0, i1,
                           jnp.where(lane == 1, i2,
                                     jnp.where(lane == 2, w1, jnp.where(lane == 3, w2, 0.0))))


def router(h, g, wr):
    T, D = h.shape
    tm = min(512, T)
    return pl.pallas_call(
        _router_kernel, grid=(T // tm,),
        in_specs=[pl.BlockSpec((tm, D), lambda i: (i, 0)), pl.BlockSpec((1, D), lambda i: (0, 0)),
                  pl.BlockSpec((D, LANES), lambda i: (0, 0))],
        out_specs=pl.BlockSpec((tm, LANES), lambda i: (i, 0)),
        out_shape=jax.ShapeDtypeStruct((T, LANES), F32),
        compiler_params=_cparams(("parallel",)), name="moe_router")(h, g, wr)


def _row_copy(src, s, dst, d, sem):
    return pltpu.make_async_copy(src.at[pl.ds(s, 1)], dst.at[pl.ds(d, 1)], sem)


def _dispatch_kernel(dest_ref, h_ref, xs_in_ref, xs_ref, sem):
    del xs_in_ref
    tm = h_ref.shape[0]

    def issue(r, c):
        _row_copy(h_ref, r, xs_ref, dest_ref[0, 0, 2 * r], sem).start()
        _row_copy(h_ref, r, xs_ref, dest_ref[0, 0, 2 * r + 1], sem).start()
        return c

    lax.fori_loop(0, tm, issue, 0, unroll=8)
    for _ in range(2):
        pltpu.make_async_copy(h_ref, xs_ref.at[pl.ds(0, tm)], sem).wait()


def dispatch(h, dest, xs_zero):
    T, D = h.shape
    tm = min(512, T)
    dest3 = dest.reshape(T // tm, 1, 2 * tm)
    return pl.pallas_call(
        _dispatch_kernel, grid=(T // tm,),
        in_specs=[pl.BlockSpec((1, 1, 2 * tm), lambda i: (i, 0, 0), memory_space=pltpu.SMEM),
                  pl.BlockSpec((tm, D), lambda i: (i, 0)),
                  pl.BlockSpec(memory_space=pl.ANY)],
        out_specs=pl.BlockSpec(memory_space=pl.ANY),
        out_shape=jax.ShapeDtypeStruct(xs_zero.shape, F32),
        scratch_shapes=[pltpu.SemaphoreType.DMA(())],
        input_output_aliases={2: 0},
        compiler_params=_cparams(("arbitrary",)), name="moe_dispatch")(dest3, h, xs_zero)


def _combine_kernel(dest_ref, dnext_ref, h_ref, route_ref, *rest, out_norm):
    og_ref = rest[0] if out_norm else None
    ys_ref, o_ref, b0_ref, b1_ref, sem = rest[1:] if out_norm else rest
    i, n = pl.program_id(0), pl.num_programs(0)
    tm = h_ref.shape[0]
    slot = i % 2

    def gather(d_ref, s):
        def issue(r, c):
            _row_copy(ys_ref, d_ref[0, 0, 2 * r], b0_ref.at[s], r, sem.at[s]).start()
            _row_copy(ys_ref, d_ref[0, 0, 2 * r + 1], b1_ref.at[s], r, sem.at[s]).start()
            return c

        lax.fori_loop(0, tm, issue, 0, unroll=8)

    @pl.when(i == 0)
    def _():
        gather(dest_ref, 0)

    @pl.when(i + 1 < n)
    def _():
        gather(dnext_ref, 1 - slot)

    pltpu.make_async_copy(ys_ref.at[pl.ds(0, tm)], b0_ref.at[slot], sem.at[slot]).wait()
    pltpu.make_async_copy(ys_ref.at[pl.ds(0, tm)], b1_ref.at[slot], sem.at[slot]).wait()
    route = route_ref[...]
    out = h_ref[...] + route[:, 2:3] * b0_ref[slot] + route[:, 3:4] * b1_ref[slot]
    o_ref[...] = _rms(out, og_ref[...]) if out_norm else out


def combine(h, route, dest, ys, out_g=None):
    T, D = h.shape
    tm = min(256, T)
    n = T // tm
    dest3 = dest.reshape(n, 1, 2 * tm)
    smem = lambda f: pl.BlockSpec((1, 1, 2 * tm), f, memory_space=pltpu.SMEM)
    out_norm = out_g is not None
    norm_spec = [pl.BlockSpec((1, D), lambda i: (0, 0))] if out_norm else []
    norm_arg = [out_g] if out_norm else []
    return pl.pallas_call(
        functools.partial(_combine_kernel, out_norm=out_norm), grid=(n,),
        in_specs=[smem(lambda i: (i, 0, 0)), smem(lambda i: (jnp.minimum(i + 1, n - 1), 0, 0)),
                  pl.BlockSpec((tm, D), lambda i: (i, 0)),
                  pl.BlockSpec((tm, LANES), lambda i: (i, 0))] + norm_spec +
                 [pl.BlockSpec(memory_space=pl.ANY)],
        out_specs=pl.BlockSpec((tm, D), lambda i: (i, 0)),
        out_shape=jax.ShapeDtypeStruct((T, D), F32),
        scratch_shapes=[pltpu.VMEM((2, tm, D), F32), pltpu.VMEM((2, tm, D), F32),
                        pltpu.SemaphoreType.DMA((2,))],
        compiler_params=_cparams(("arbitrary",)), name="moe_combine")(dest3, dest3, h, route, *norm_arg, ys)


def moe_ffn(h, g, wr, wg, wu, wd, layer, xs_buf=None, out_g=None):
    T, D = h.shape
    tm = min(1024, T)
    route = router(h, g, wr)
    e_flat = route[:, :2].astype(jnp.int32).reshape(2 * T)
    onehot = (e_flat[:, None] == jnp.arange(N_EXPERTS, dtype=jnp.int32)[None, :]).astype(jnp.int32)
    csum = jnp.cumsum(onehot, axis=0)
    counts = csum[-1]
    rank = jnp.sum((csum - 1) * onehot, axis=1)
    tiles = (counts + tm - 1) // tm
    tile_end = jnp.cumsum(tiles)
    offs = (tile_end - tiles) * tm
    dest = (jnp.sum(onehot * offs[None, :], axis=1) + rank).astype(jnp.int32)
    n_tiles = (2 * T) // tm + N_EXPERTS
    n_used = tile_end[-1].astype(jnp.int32)
    tile_ids = jnp.minimum(jnp.arange(n_tiles, dtype=jnp.int32), n_used - 1)
    tile_expert = jnp.sum((tile_ids[:, None] >= tile_end[None, :]).astype(jnp.int32), axis=1).astype(jnp.int32)
    if xs_buf is None:
        xs_buf = jnp.zeros((n_tiles * tm, D), F32)
    xs = dispatch(h, dest, xs_buf)
    ys = ffn_grouped(xs, g, wg, wu, wd, layer, tile_expert, n_used.reshape(1), tm)
    return combine(h, route, dest, ys, out_g), xs


def _final_norm_kernel(h_ref, g_ref, o_ref):
    o_ref[...] = _rms(h_ref[...], g_ref[...])


def final_rms(h, g):
    T, D = h.shape
    tm = min(1024, T)
    return pl.pallas_call(
        _final_norm_kernel, grid=(T // tm,),
        in_specs=[pl.BlockSpec((tm, D), lambda i: (i, 0)), pl.BlockSpec((1, D), lambda i: (0, 0))],
        out_specs=pl.BlockSpec((tm, D), lambda i: (i, 0)),
        out_shape=jax.ShapeDtypeStruct((T, D), F32),
        compiler_params=_cparams(("parallel",)), name="final_norm")(h, g)


def kernel(x, positions, attn_norm, w_in, mla_q_norm, mla_w_uq, mla_kv_norm, mla_w_ukv, conv_w, gmlp_ln, gmlp_w_s, gmlp_b_s, nsa_pe_k, nsa_w1_k, nsa_w2_k, nsa_pe_v, nsa_w1_v, nsa_w2_v, group_norm, w_o, ffn_norm, dense_w_gate, dense_w_up, dense_w_down, moe_router, moe_w_gate, moe_w_up, moe_w_down, final_norm):
    B, S, D = x.shape
    T = B * S
    depth = w_in.shape[0]
    h = x.reshape(T, D)
    pos = positions.astype(F32).reshape(T, 1)

    w_in_p = prep_w_in(w_in)
    wqn, wukt, wra, wrb, wuvp, freq, sign = prep_mla(mla_w_uq, mla_w_ukv)
    rope_cos, rope_sin = rope_tables(pos, freq, sign)
    conv_w8 = jnp.pad(conv_w, ((0, 0), (0, 8 - CONV_K), (0, 0)))
    b_exp = jnp.repeat(jnp.swapaxes(gmlp_b_s, 1, 2), GROUP_W // GMLP_GROUPS, axis=2)
    sel, place = _head_select(NSA_HEADS)
    msel = cmp_to_sel(S)
    w_o_b = w_o.astype(BF16)
    wr_p = jnp.pad(moe_router, ((0, 0), (0, 0), (0, LANES - N_EXPERTS)))
    moe_wg, moe_wu, moe_wd = moe_w_gate.astype(BF16), moe_w_up.astype(BF16), moe_w_down.astype(BF16)
    dense_wg, dense_wu, dense_wd = (dense_w_gate.astype(BF16), dense_w_up.astype(BF16),
                                    dense_w_down.astype(BF16))
    xs_buf = None

    for l in range(depth):
        x_mla, x_conv, x_gmlp, x_nq, x_cmp, x_sel, x_win, x_gate = in_proj(h, attn_norm[l][None, :], w_in_p[l])

        q, k = mla_proj(x_mla, rope_cos, rope_sin, mla_q_norm[l][None, :], mla_kv_norm[l][None, :],
                        wqn[l], wukt[l], wra[l], wrb[l])
        y_mla = mla_attn(q, k, wuvp[l], B, S)

        y_cg = conv_gmlp(x_conv, x_gmlp, conv_w8[l], gmlp_ln[l][None, :], gmlp_w_s[l], b_exp[l], S)

        pea, peb, w1a, w1b, w2 = prep_compress(nsa_pe_k[l], nsa_w1_k[l], nsa_w2_k[l],
                                               nsa_pe_v[l], nsa_w1_v[l], nsa_w2_v[l])
        z2 = x_cmp.reshape(B, S // CMP_STRIDE, CMP_STRIDE * LANES)
        kcv = compress(z2, pea, peb, w1a, w1b, w2)
        o_cmp, sel_bias = nsa_cmp(x_nq, kcv, x_gate, msel, sel, place, B, S)
        y_nsa = nsa_attn(x_nq, x_sel, x_win, sel_bias, o_cmp, x_gate, sel, place, B, S)

        h = mix_out(y_mla, y_cg, y_nsa, h, group_norm[l].reshape(1, D), w_o_b[l])

        fg = ffn_norm[l][None, :]
        if l % 2 == 0:
            h = ffn_dense(h, fg, dense_wg, dense_wu, dense_wd, l // 2)
        else:
            out_g = final_norm[None, :] if l == depth - 1 else None
            h, xs_buf = moe_ffn(h, fg, wr_p[l // 2], moe_wg, moe_wu, moe_wd, l // 2, xs_buf, out_g)
    if depth % 2 == 1:
        h = final_rms(h, final_norm[None, :])
    return h.reshape(B, S, D)
```
